```python
import jax, jax.numpy as jnp
from jax import lax
import numpy as np

D_MODEL = 1024
BATCH = 16
SEQ = 2048
DEPTH = 1

CHUNK = 64
D_HEAD = 64
A_HEADS = 8
A_PREV_CHUNKS = 8
MAX_REL = 128
B_Q_HEADS = 8
B_KV_HEADS = 2
B_GROUP = B_Q_HEADS // B_KV_HEADS
B_WINDOW = 128
B_PREV_CHUNKS = B_WINDOW // CHUNK
A_WIDTH = A_HEADS * D_HEAD
B_Q_WIDTH = B_Q_HEADS * D_HEAD
B_KV_WIDTH = B_KV_HEADS * D_HEAD
D_FF = 2816
REL_TABLE = (CHUNK - 1) + MAX_REL + 1
IN_WIDTH = 3 * A_WIDTH + B_Q_WIDTH + 2 * B_KV_WIDTH + 2 * D_MODEL
EPS = 1e-6
NEG_INF = -1e30

kernel_name = "streaming_hybrid_gated_chunk_attention_block"


def rms_norm(x, g):
    xf = x.astype(jnp.float32)
    y = xf * lax.rsqrt(jnp.mean(xf * xf, axis=-1, keepdims=True) + EPS)
    return y.astype(x.dtype) * g


def swiglu(h, w_gate, w_up, w_down):
    return (jax.nn.silu(h @ w_gate) * (h @ w_up)) @ w_down


def alibi_slopes(n):
    return np.array([2.0 ** (-8.0 * (i + 1) / n) for i in range(n)], dtype=np.float32)


def banded_chunk_attention(q, k, v, n_prev, bias, sinks=None):
    b, s, hkv, g, dh = q.shape
    n_chunks = s // CHUNK
    pad = n_prev * CHUNK
    band = (n_prev + 1) * CHUNK
    scale = 1.0 / np.sqrt(dh)
    kp = jnp.pad(k, ((0, 0), (pad, 0), (0, 0), (0, 0)))
    vp = jnp.pad(v, ((0, 0), (pad, 0), (0, 0), (0, 0)))
    key_off = jnp.arange(band)

    def one_chunk(c):
        start = c * CHUNK
        qc = lax.dynamic_slice_in_dim(q, start, CHUNK, axis=1)
        kc = lax.dynamic_slice_in_dim(kp, start, band, axis=1)
        vc = lax.dynamic_slice_in_dim(vp, start, band, axis=1)
        sc = jnp.einsum('bqhgd,bkhd->bhgqk', qc, kc).astype(jnp.float32) * scale + bias
        valid = (start - pad + key_off) >= 0
        sc = jnp.where(valid, sc, NEG_INF)
        if sinks is not None:
            sink_col = jnp.broadcast_to(sinks.astype(jnp.float32)[None, :, :, None, None],
                                        sc.shape[:-1] + (1,))
            p = jax.nn.softmax(jnp.concatenate([sc, sink_col], axis=-1), axis=-1)[..., :band]
        else:
            p = jax.nn.softmax(sc, axis=-1)
        return jnp.einsum('bhgqk,bkhd->bqhgd', p.astype(vc.dtype), vc)

    out = lax.map(one_chunk, jnp.arange(n_chunks))
    return jnp.moveaxis(out, 0, 1).reshape(b, s, hkv * g * dh)


def setup_inputs(seed: int = 0) -> dict:
    key = jax.random.key(seed)
    ks = jax.random.split(key, 20)
    f32 = jnp.float32

    def w(k, shape, fan_in):
        return jax.random.normal(k, shape, f32) * fan_in ** -0.5

    def gain(k):
        return 1.0 + 0.01 * jax.random.normal(k, (D_MODEL,), f32)

    return {
        "x": jax.random.normal(ks[0], (BATCH, SEQ, D_MODEL), f32),
        "ffn1_norm": gain(ks[1]),
        "ffn1_w_gate": w(ks[2], (D_MODEL, D_FF), D_MODEL),
        "ffn1_w_up": w(ks[3], (D_MODEL, D_FF), D_MODEL),
        "ffn1_w_down": w(ks[4], (D_FF, D_MODEL), D_FF),
        "mix_norm": gain(ks[5]),
        "w_in": w(ks[6], (D_MODEL, IN_WIDTH), D_MODEL),
        "rel_bias": 0.5 * jax.random.normal(ks[7], (A_HEADS, REL_TABLE), f32),
        "sinks": jax.random.normal(ks[8], (B_Q_HEADS,), f32),
        "w_proj_a": w(ks[9], (A_WIDTH, D_MODEL), A_WIDTH),
        "w_proj_b": w(ks[10], (B_Q_WIDTH, D_MODEL), B_Q_WIDTH),
        "w_out": w(ks[11], (D_MODEL, D_MODEL), D_MODEL),
        "ffn2_norm": gain(ks[12]),
        "ffn2_w_gate": w(ks[13], (D_MODEL, D_FF), D_MODEL),
        "ffn2_w_up": w(ks[14], (D_MODEL, D_FF), D_MODEL),
        "ffn2_w_down": w(ks[15], (D_FF, D_MODEL), D_FF),
        "final_norm": gain(ks[16]),
    }


def reference(x, ffn1_norm, ffn1_w_gate, ffn1_w_up, ffn1_w_down, mix_norm, w_in,
              rel_bias, sinks, w_proj_a, w_proj_b, w_out, ffn2_norm, ffn2_w_gate,
              ffn2_w_up, ffn2_w_down, final_norm):
    b, s, _ = x.shape

    qi = np.arange(CHUNK)[:, None]
    kj_a = np.arange((A_PREV_CHUNKS + 1) * CHUNK)[None, :]
    rel_a = qi - kj_a + A_PREV_CHUNKS * CHUNK
    idx_a = np.clip(rel_a, -(CHUNK - 1), MAX_REL) + (CHUNK - 1)
    kj_b = np.arange((B_PREV_CHUNKS + 1) * CHUNK)[None, :]
    dist_b = np.abs(qi - kj_b + B_PREV_CHUNKS * CHUNK).astype(np.float32)
    slopes = jnp.asarray(alibi_slopes(B_Q_HEADS)).reshape(B_KV_HEADS, B_GROUP)

    for _layer in range(DEPTH):
        x = x + 0.5 * swiglu(rms_norm(x, ffn1_norm), ffn1_w_gate, ffn1_w_up, ffn1_w_down)

        h = rms_norm(x, mix_norm)
        proj = h @ w_in
        cuts = np.cumsum([A_WIDTH, A_WIDTH, A_WIDTH, B_Q_WIDTH, B_KV_WIDTH, B_KV_WIDTH, D_MODEL])
        qa, ka, va, qb, kb, vb, gate_a, gate_b = jnp.split(proj, cuts, axis=-1)

        bias_a = rel_bias[:, idx_a].astype(jnp.float32)[:, None]
        ya = banded_chunk_attention(
            qa.reshape(b, s, A_HEADS, 1, D_HEAD),
            ka.reshape(b, s, A_HEADS, D_HEAD),
            va.reshape(b, s, A_HEADS, D_HEAD),
            A_PREV_CHUNKS, bias_a)
        ya = ya @ w_proj_a

        bias_b = -slopes[:, :, None, None] * jnp.asarray(dist_b)[None, None]
        yb = banded_chunk_attention(
            qb.reshape(b, s, B_KV_HEADS, B_GROUP, D_HEAD),
            kb.reshape(b, s, B_KV_HEADS, D_HEAD),
            vb.reshape(b, s, B_KV_HEADS, D_HEAD),
            B_PREV_CHUNKS, bias_b, sinks.reshape(B_KV_HEADS, B_GROUP))
        yb = yb @ w_proj_b

        merged = jax.nn.sigmoid(gate_a) * ya + jax.nn.sigmoid(gate_b) * yb
        x = x + merged @ w_out

        x = x + 0.5 * swiglu(rms_norm(x, ffn2_norm), ffn2_w_gate, ffn2_w_up, ffn2_w_down)

    return rms_norm(x, final_norm)
```

```python
import functools

import numpy as np
import jax
import jax.numpy as jnp
from jax import lax
from jax.experimental import pallas as pl
from jax.experimental.pallas import tpu as pltpu

F32 = jnp.float32
BF16 = jnp.bfloat16

D_MODEL = 1024
D_FF = 2816
CHUNK = 64
D_HEAD = 64
A_HEADS = 8
A_PREV_CHUNKS = 8
MAX_REL = 128
B_Q_HEADS = 8
B_KV_HEADS = 2
B_GROUP = B_Q_HEADS // B_KV_HEADS
B_PREV_CHUNKS = 2
A_WIDTH = A_HEADS * D_HEAD
B_Q_WIDTH = B_Q_HEADS * D_HEAD
B_KV_WIDTH = B_KV_HEADS * D_HEAD
QKV_A_WIDTH = 3 * A_WIDTH
QKV_B_WIDTH = B_Q_WIDTH + 2 * B_KV_WIDTH
GATES_WIDTH = 2 * D_MODEL
EPS = 1e-6
NEG_INF = -1e30

LANES = 128
MXU_DIM = 256
VMEM_LIMIT_BYTES = 56 * 1024 * 1024

ROW_TILE = 512
FF_CHUNK = MXU_DIM
A_QBLOCK = 4 * CHUNK
A_KEY_TILES = A_PREV_CHUNKS * CHUNK // A_QBLOCK + 1
B_QBLOCK = 2 * CHUNK
B_WINDOW = 2 * B_QBLOCK


def _rms_norm(x, gain):
    return x * lax.rsqrt(jnp.mean(x * x, axis=-1, keepdims=True) + EPS) * gain


def _swiglu(h, wg_ref, wu_ref, wd_ref, hid_ref):
    for c in range(D_FF // FF_CHUNK):
        cols = slice(c * FF_CHUNK, (c + 1) * FF_CHUNK)
        g = jnp.dot(h, wg_ref[:, cols], preferred_element_type=F32)
        u = jnp.dot(h, wu_ref[:, cols], preferred_element_type=F32)
        hid_ref[:, cols] = (g * jax.nn.sigmoid(g) * u).astype(BF16)
    return jnp.dot(hid_ref[...], wd_ref[...], preferred_element_type=F32)


def _ffn_in_kernel(x_ref, g1_ref, wg_ref, wu_ref, wd_ref, gm_ref, wa_ref, wb_ref, wgt_ref,
                   x1_ref, qkva_ref, qkvb_ref, gates_ref, hid_ref):
    x = x_ref[...]
    h = _rms_norm(x, g1_ref[...]).astype(BF16)
    x1 = x + 0.5 * _swiglu(h, wg_ref, wu_ref, wd_ref, hid_ref)
    x1_ref[...] = x1
    h2 = _rms_norm(x1, gm_ref[...]).astype(BF16)
    qkva_ref[...] = jnp.dot(h2, wa_ref[...], preferred_element_type=F32).astype(BF16)
    qkvb_ref[...] = jnp.dot(h2, wb_ref[...], preferred_element_type=F32).astype(BF16)
    gates_ref[...] = jnp.dot(h2, wgt_ref[...], preferred_element_type=F32).astype(BF16)


def _ffn_out_kernel(x1_ref, ya_ref, yb_ref, ga_ref, gb_ref, wpa_ref, wpb_ref, wo_ref,
                    g2_ref, wg_ref, wu_ref, wd_ref, gf_ref, out_ref, hid_ref):
    pa = jnp.dot(ya_ref[...], wpa_ref[...], preferred_element_type=F32)
    pb = jnp.dot(yb_ref[...], wpb_ref[...], preferred_element_type=F32)
    merged = (jax.nn.sigmoid(ga_ref[...].astype(F32)) * pa
              + jax.nn.sigmoid(gb_ref[...].astype(F32)) * pb)
    x2 = x1_ref[...] + jnp.dot(merged.astype(BF16), wo_ref[...], preferred_element_type=F32)
    h = _rms_norm(x2, g2_ref[...]).astype(BF16)
    x3 = x2 + 0.5 * _swiglu(h, wg_ref, wu_ref, wd_ref, hid_ref)
    out_ref[...] = _rms_norm(x3, gf_ref[...])


def _dot_nt(a, b):
    return lax.dot_general(a, b, (((1,), (1,)), ((), ())), preferred_element_type=F32)


def _attn_a_kernel(q_ref, k_ref, v_ref, bias_ref, o_ref):
    seq = q_ref.shape[0]
    low = lax.broadcasted_iota(jnp.int32, (1, LANES), 1) < D_HEAD
    for m in range(seq // A_QBLOCK):
        q = q_ref[m * A_QBLOCK:(m + 1) * A_QBLOCK, :]
        dists = [d for d in range(A_KEY_TILES - 1, -1, -1) if m - d >= 0]
        outs = []
        for hh in range(2):
            qh = jnp.where(low if hh == 0 else ~low, q, jnp.zeros_like(q))
            s = []
            for d in dists:
                rows = slice((m - d) * A_QBLOCK, (m - d + 1) * A_QBLOCK)
                s.append(_dot_nt(qh, k_ref[rows, :]) + bias_ref[hh, d])
            mx = functools.reduce(jnp.maximum, [jnp.max(t, axis=-1, keepdims=True) for t in s])
            p = [jnp.exp(t - mx) for t in s]
            denom = functools.reduce(jnp.add, [jnp.sum(t, axis=-1, keepdims=True) for t in p])
            acc = None
            for d, t in zip(dists, p):
                rows = slice((m - d) * A_QBLOCK, (m - d + 1) * A_QBLOCK)
                pv = jnp.dot(t.astype(BF16), v_ref[rows, :], preferred_element_type=F32)
                acc = pv if acc is None else acc + pv
            outs.append(acc / denom)
        o_ref[m * A_QBLOCK:(m + 1) * A_QBLOCK, :] = jnp.where(low, outs[0], outs[1]).astype(BF16)


def _attn_b_kernel(sinks_ref, q_ref, k_ref, v_ref, bias_ref, o_ref):
    seq = q_ref.shape[0]
    low = lax.broadcasted_iota(jnp.int32, (1, LANES), 1) < D_HEAD
    k_all = k_ref[...]
    k_half = [jnp.where(low if g == 0 else ~low, k_all, jnp.zeros_like(k_all))
              for g in range(B_KV_HEADS)]
    sink = [jnp.concatenate([jnp.full((B_QBLOCK, 1), sinks_ref[g * B_GROUP + c], F32)
                             for c in range(B_GROUP)], axis=0) for g in range(B_KV_HEADS)]
    for m in range(seq // B_QBLOCK):
        qrows = slice(m * B_QBLOCK, (m + 1) * B_QBLOCK)
        krows = slice(max(m - 1, 0) * B_QBLOCK, (m + 1) * B_QBLOCK)
        bcols = slice(B_WINDOW - (krows.stop - krows.start), B_WINDOW)
        qs = jnp.concatenate([q_ref[qrows, c * LANES:(c + 1) * LANES] for c in range(B_GROUP)],
                             axis=0)
        outs = []
        for g in range(B_KV_HEADS):
            s = _dot_nt(qs, k_half[g][krows, :]) + bias_ref[g, :, bcols]
            mx = jnp.maximum(jnp.max(s, axis=-1, keepdims=True), sink[g])
            p = jnp.exp(s - mx)
            denom = jnp.sum(p, axis=-1, keepdims=True) + jnp.exp(sink[g] - mx)
            pv = jnp.dot(p.astype(BF16), v_ref[krows, :], preferred_element_type=F32)
            outs.append(pv / denom)
        merged = jnp.where(low, outs[0], outs[1]).astype(BF16)
        for c in range(B_GROUP):
            o_ref[qrows, c * LANES:(c + 1) * LANES] = merged[c * B_QBLOCK:(c + 1) * B_QBLOCK, :]


def _resident(shape):
    return pl.BlockSpec(shape, lambda *_: (0,) * len(shape), pipeline_mode=pl.Buffered(1))


def _bias_a_index():
    i = np.arange(A_QBLOCK)[:, None]
    j = np.arange(A_QBLOCK)[None, :]
    idx, vis = [], []
    for d in range(A_KEY_TILES):
        rel = d * A_QBLOCK + i - j
        chunk_diff = d * (A_QBLOCK // CHUNK) + i // CHUNK - j // CHUNK
        idx.append(np.clip(rel, -(CHUNK - 1), MAX_REL) + (CHUNK - 1))
        vis.append((chunk_diff >= 0) & (chunk_diff <= A_PREV_CHUNKS))
    return np.stack(idx), np.stack(vis)


def _bias_b_table():
    slopes = np.array([2.0 ** (-8.0 * (h + 1) / B_Q_HEADS) for h in range(B_Q_HEADS)], np.float32)
    i = np.arange(B_QBLOCK)[:, None]
    j = np.arange(B_WINDOW)[None, :]
    rel = i - j + B_QBLOCK
    chunk_diff = i // CHUNK - j // CHUNK + B_QBLOCK // CHUNK
    vis = (chunk_diff >= 0) & (chunk_diff <= B_PREV_CHUNKS)
    dist = np.abs(rel).astype(np.float32)
    out = np.empty((B_KV_HEADS, B_GROUP * B_QBLOCK, B_WINDOW), np.float32)
    for g in range(B_KV_HEADS):
        for c in range(B_GROUP):
            out[g, c * B_QBLOCK:(c + 1) * B_QBLOCK] = np.where(
                vis, -slopes[g * B_GROUP + c] * dist, NEG_INF)
    return out


def kernel(x, ffn1_norm, ffn1_w_gate, ffn1_w_up, ffn1_w_down, mix_norm, w_in, rel_bias, sinks,
           w_proj_a, w_proj_b, w_out, ffn2_norm, ffn2_w_gate, ffn2_w_up, ffn2_w_down, final_norm):
    batch, seq, d = x.shape
    rows = batch * seq
    assert d == D_MODEL and rows % ROW_TILE == 0 and seq % A_QBLOCK == 0
    n_row_tiles = rows // ROW_TILE
    scale = 1.0 / np.sqrt(D_HEAD)

    b_perm = np.concatenate([np.arange(h * D_HEAD, (h + 1) * D_HEAD)
                             for c in range(B_GROUP) for h in (c, B_GROUP + c)])
    cuts = np.cumsum([A_WIDTH, A_WIDTH, A_WIDTH, B_Q_WIDTH, B_KV_WIDTH, B_KV_WIDTH, D_MODEL])
    wqa, wka, wva, wqb, wkb, wvb, wga, wgb = jnp.split(w_in, cuts, axis=1)
    w_qkva = jnp.concatenate([wqa * scale, wka, wva], axis=1).astype(BF16)
    w_qkvb = jnp.concatenate([wqb[:, b_perm] * scale, wkb, wvb], axis=1).astype(BF16)
    w_gates = jnp.concatenate([wga, wgb], axis=1).astype(BF16)
    w_pb = w_proj_b[b_perm, :].astype(BF16)
    vec = lambda g: g.reshape(1, D_MODEL).astype(F32)

    row_block = lambda width: pl.BlockSpec((ROW_TILE, width), lambda i: (i, 0))
    dense_params = pltpu.CompilerParams(dimension_semantics=("arbitrary",),
                                        vmem_limit_bytes=VMEM_LIMIT_BYTES)

    x1, qkva, qkvb, gates = pl.pallas_call(
        _ffn_in_kernel,
        grid=(n_row_tiles,),
        in_specs=[row_block(D_MODEL), _resident((1, D_MODEL)),
                  _resident((D_MODEL, D_FF)), _resident((D_MODEL, D_FF)),
                  _resident((D_FF, D_MODEL)), _resident((1, D_MODEL)),
                  _resident((D_MODEL, QKV_A_WIDTH)), _resident((D_MODEL, QKV_B_WIDTH)),
                  _resident((D_MODEL, GATES_WIDTH))],
        out_specs=[row_block(D_MODEL), row_block(QKV_A_WIDTH), row_block(QKV_B_WIDTH),
                   row_block(GATES_WIDTH)],
        out_shape=[jax.ShapeDtypeStruct((rows, D_MODEL), F32),
                   jax.ShapeDtypeStruct((rows, QKV_A_WIDTH), BF16),
                   jax.ShapeDtypeStruct((rows, QKV_B_WIDTH), BF16),
                   jax.ShapeDtypeStruct((rows, GATES_WIDTH), BF16)],
        scratch_shapes=[pltpu.VMEM((ROW_TILE, D_FF), BF16)],
        compiler_params=dense_params,
        name="ffn_in",
    )(x.reshape(rows, D_MODEL), vec(ffn1_norm), ffn1_w_gate.astype(BF16),
      ffn1_w_up.astype(BF16), ffn1_w_down.astype(BF16), vec(mix_norm), w_qkva, w_qkvb, w_gates)

    idx_a, vis_a = _bias_a_index()
    bias_a = jnp.where(vis_a[None], rel_bias.astype(F32)[:, idx_a], NEG_INF)
    n_pairs = A_WIDTH // LANES
    seq_block = lambda col: pl.BlockSpec((None, seq, LANES), lambda j, b, col=col: (b, 0, col + j))
    ya = pl.pallas_call(
        _attn_a_kernel,
        grid=(n_pairs, batch),
        in_specs=[seq_block(0), seq_block(n_pairs), seq_block(2 * n_pairs),
                  pl.BlockSpec((2, A_KEY_TILES, A_QBLOCK, A_QBLOCK), lambda j, b: (j, 0, 0, 0))],
        out_specs=seq_block(0),
        out_shape=jax.ShapeDtypeStruct((batch, seq, A_WIDTH), BF16),
        compiler_params=pltpu.CompilerParams(dimension_semantics=("arbitrary", "arbitrary"),
                                             vmem_limit_bytes=VMEM_LIMIT_BYTES),
        name="attn_a",
    )(*([qkva.reshape(batch, seq, QKV_A_WIDTH)] * 3), bias_a)

    qkvb3 = qkvb.reshape(batch, seq, QKV_B_WIDTH)
    kv_col = B_Q_WIDTH // LANES
    yb = pl.pallas_call(
        _attn_b_kernel,
        grid=(batch,),
        in_specs=[pl.BlockSpec(memory_space=pltpu.SMEM),
                  pl.BlockSpec((None, seq, B_Q_WIDTH), lambda b: (b, 0, 0)),
                  pl.BlockSpec((None, seq, LANES), lambda b: (b, 0, kv_col)),
                  pl.BlockSpec((None, seq, LANES), lambda b: (b, 0, kv_col + 1)),
                  _resident((B_KV_HEADS, B_GROUP * B_QBLOCK, B_WINDOW))],
        out_specs=pl.BlockSpec((None, seq, B_Q_WIDTH), lambda b: (b, 0, 0)),
        out_shape=jax.ShapeDtypeStruct((batch, seq, B_Q_WIDTH), BF16),
        compiler_params=pltpu.CompilerParams(dimension_semantics=("arbitrary",),
                                             vmem_limit_bytes=VMEM_LIMIT_BYTES),
        name="attn_b",
    )(sinks.astype(F32), qkvb3, qkvb3, qkvb3, jnp.asarray(_bias_b_table()))

    gate_block = lambda col: pl.BlockSpec((ROW_TILE, D_MODEL), lambda i, col=col: (i, col))
    out = pl.pallas_call(
        _ffn_out_kernel,
        grid=(n_row_tiles,),
        in_specs=[row_block(D_MODEL), row_block(A_WIDTH), row_block(B_Q_WIDTH),
                  gate_block(0), gate_block(1),
                  _resident((A_WIDTH, D_MODEL)), _resident((B_Q_WIDTH, D_MODEL)),
                  _resident((D_MODEL, D_MODEL)), _resident((1, D_MODEL)),
                  _resident((D_MODEL, D_FF)), _resident((D_MODEL, D_FF)),
                  _resident((D_FF, D_MODEL)), _resident((1, D_MODEL))],
        out_specs=row_block(D_MODEL),
        out_shape=jax.ShapeDtypeStruct((rows, D_MODEL), F32),
        scratch_shapes=[pltpu.VMEM((ROW_TILE, D_FF), BF16)],
        compiler_params=dense_params,
        name="ffn_out",
    )(x1, ya.reshape(rows, A_WIDTH), yb.reshape(rows, B_Q_WIDTH), gates, gates,
      w_proj_a.astype(BF16), w_pb, w_out.astype(BF16), vec(ffn2_norm),
      ffn2_w_gate.astype(BF16), ffn2_w_up.astype(BF16), ffn2_w_down.astype(BF16),
      vec(final_norm))
    return out.reshape(batch, seq, D_MODEL)
```

```python
import functools

import numpy as np
import jax
import jax.numpy as jnp
from jax import lax
from jax.experimental import pallas as pl
from jax.experimental.pallas import tpu as pltpu

F32 = jnp.float32
BF16 = jnp.bfloat16

D_MODEL = 1024
D_FF = 2816
CHUNK = 64
D_HEAD = 64
A_HEADS = 8
A_PREV_CHUNKS = 8
MAX_REL = 128
B_Q_HEADS = 8
B_KV_HEADS = 2
B_GROUP = B_Q_HEADS // B_KV_HEADS
B_PREV_CHUNKS = 2
A_WIDTH = A_HEADS * D_HEAD
B_Q_WIDTH = B_Q_HEADS * D_HEAD
B_KV_WIDTH = B_KV_HEADS * D_HEAD
QKV_A_WIDTH = 3 * A_WIDTH
QKV_B_WIDTH = B_Q_WIDTH + 2 * B_KV_WIDTH
GATES_WIDTH = 2 * D_MODEL
EPS = 1e-6
NEG_INF = -1e30

LANES = 128
MXU_DIM = 256
VMEM_LIMIT_BYTES = 56 * 1024 * 1024

ROW_TILE = 512
FF_CHUNK = MXU_DIM
A_QBLOCK = 4 * CHUNK
A_KEY_TILES = A_PREV_CHUNKS * CHUNK // A_QBLOCK + 1
B_QBLOCK = 2 * CHUNK
B_WINDOW = 2 * B_QBLOCK


def _rms_norm(x, gain):
    return x * lax.rsqrt(jnp.mean(x * x, axis=-1, keepdims=True) + EPS) * gain


def _swiglu(h, wg_ref, wu_ref, wd_ref, hid_ref):
    for c in range(D_FF // FF_CHUNK):
        cols = slice(c * FF_CHUNK, (c + 1) * FF_CHUNK)
        g = jnp.dot(h, wg_ref[:, cols], preferred_element_type=F32)
        u = jnp.dot(h, wu_ref[:, cols], preferred_element_type=F32)
        hid_ref[:, cols] = (g * jax.nn.sigmoid(g) * u).astype(BF16)
    return jnp.dot(hid_ref[...], wd_ref[...], preferred_element_type=F32)


def _ffn_in_kernel(x_ref, g1_ref, wg_ref, wu_ref, wd_ref, gm_ref, wa_ref, wb_ref, wgt_ref,
                   x1_ref, qkva_ref, qkvb_ref, gates_ref, hid_ref):
    x = x_ref[...]
    h = _rms_norm(x, g1_ref[...]).astype(BF16)
    x1 = x + 0.5 * _swiglu(h, wg_ref, wu_ref, wd_ref, hid_ref)
    x1_ref[...] = x1
    h2 = _rms_norm(x1, gm_ref[...]).astype(BF16)
    qkva_ref[...] = jnp.dot(h2, wa_ref[...], preferred_element_type=F32).astype(BF16)
    qkvb_ref[...] = jnp.dot(h2, wb_ref[...], preferred_element_type=F32).astype(BF16)
    gates_ref[...] = jnp.dot(h2, wgt_ref[...], preferred_element_type=F32).astype(BF16)


def _ffn_out_kernel(x1_ref, ya_ref, yb_ref, ga_ref, gb_ref, wpa_ref, wpb_ref, wo_ref,
                    g2_ref, wg_ref, wu_ref, wd_ref, gf_ref, out_ref, hid_ref):
    pa = jnp.dot(ya_ref[...], wpa_ref[...], preferred_element_type=F32)
    pb = jnp.dot(yb_ref[...], wpb_ref[...], preferred_element_type=F32)
    merged = (jax.nn.sigmoid(ga_ref[...].astype(F32)) * pa
              + jax.nn.sigmoid(gb_ref[...].astype(F32)) * pb)
    x2 = x1_ref[...] + jnp.dot(merged.astype(BF16), wo_ref[...], preferred_element_type=F32)
    h = _rms_norm(x2, g2_ref[...]).astype(BF16)
    x3 = x2 + 0.5 * _swiglu(h, wg_ref, wu_ref, wd_ref, hid_ref)
    out_ref[...] = _rms_norm(x3, gf_ref[...])


def _dot_nt(a, b):
    return lax.dot_general(a, b, (((1,), (1,)), ((), ())), preferred_element_type=F32)


def _attn_a_kernel(q_ref, k_ref, v_ref, tab_ref, o_ref, bias_ref):
    seq = q_ref.shape[0]
    low = lax.broadcasted_iota(jnp.int32, (1, LANES), 1) < D_HEAD

    @pl.when(pl.program_id(1) == 0)
    def _build_bias():
        qc = lax.broadcasted_iota(jnp.int32, (A_QBLOCK, A_QBLOCK), 0) // CHUNK
        kc = lax.broadcasted_iota(jnp.int32, (A_QBLOCK, A_QBLOCK), 1) // CHUNK
        for hh in range(2):
            for d in range(A_KEY_TILES):
                row = jnp.broadcast_to(tab_ref[hh, d], (A_QBLOCK, 2 * A_QBLOCK))
                toeplitz = pltpu.roll(row, 0, 1, stride=1, stride_axis=0)[:, :A_QBLOCK]
                chunk_diff = d * (A_QBLOCK // CHUNK) + qc - kc
                visible = (chunk_diff >= 0) & (chunk_diff <= A_PREV_CHUNKS)
                bias_ref[hh, d] = jnp.where(visible, toeplitz, NEG_INF)

    for m in range(seq // A_QBLOCK):
        q = q_ref[m * A_QBLOCK:(m + 1) * A_QBLOCK, :]
        dists = [d for d in range(A_KEY_TILES - 1, -1, -1) if m - d >= 0]
        outs = []
        for hh in range(2):
            qh = jnp.where(low if hh == 0 else ~low, q, jnp.zeros_like(q))
            s = []
            for d in dists:
                rows = slice((m - d) * A_QBLOCK, (m - d + 1) * A_QBLOCK)
                s.append(_dot_nt(qh, k_ref[rows, :]) + bias_ref[hh, d])
            mx = functools.reduce(jnp.maximum, [jnp.max(t, axis=-1, keepdims=True) for t in s])
            p = [jnp.exp(t - mx) for t in s]
            denom = functools.reduce(jnp.add, [jnp.sum(t, axis=-1, keepdims=True) for t in p])
            acc = None
            for d, t in zip(dists, p):
                rows = slice((m - d) * A_QBLOCK, (m - d + 1) * A_QBLOCK)
                pv = jnp.dot(t.astype(BF16), v_ref[rows, :], preferred_element_type=F32)
                acc = pv if acc is None else acc + pv
            outs.append(acc / denom)
        o_ref[m * A_QBLOCK:(m + 1) * A_QBLOCK, :] = jnp.where(low, outs[0], outs[1]).astype(BF16)


def _attn_b_kernel(sinks_ref, q_ref, k_ref, v_ref, bias_ref, o_ref):
    seq = q_ref.shape[0]
    low = lax.broadcasted_iota(jnp.int32, (1, LANES), 1) < D_HEAD
    k_all = k_ref[...]
    k_half = [jnp.where(low if g == 0 else ~low, k_all, jnp.zeros_like(k_all))
              for g in range(B_KV_HEADS)]
    sink = [jnp.concatenate([jnp.full((B_QBLOCK, 1), sinks_ref[g * B_GROUP + c], F32)
                             for c in range(B_GROUP)], axis=0) for g in range(B_KV_HEADS)]
    for m in range(seq // B_QBLOCK):
        qrows = slice(m * B_QBLOCK, (m + 1) * B_QBLOCK)
        krows = slice(max(m - 1, 0) * B_QBLOCK, (m + 1) * B_QBLOCK)
        bcols = slice(B_WINDOW - (krows.stop - krows.start), B_WINDOW)
        qs = jnp.concatenate([q_ref[qrows, c * LANES:(c + 1) * LANES] for c in range(B_GROUP)],
                             axis=0)
        outs = []
        for g in range(B_KV_HEADS):
            s = _dot_nt(qs, k_half[g][krows, :]) + bias_ref[g, :, bcols]
            mx = jnp.maximum(jnp.max(s, axis=-1, keepdims=True), sink[g])
            p = jnp.exp(s - mx)
            denom = jnp.sum(p, axis=-1, keepdims=True) + jnp.exp(sink[g] - mx)
            pv = jnp.dot(p.astype(BF16), v_ref[krows, :], preferred_element_type=F32)
            outs.append(pv / denom)
        merged = jnp.where(low, outs[0], outs[1]).astype(BF16)
        for c in range(B_GROUP):
            o_ref[qrows, c * LANES:(c + 1) * LANES] = merged[c * B_QBLOCK:(c + 1) * B_QBLOCK, :]


def _resident(shape):
    return pl.BlockSpec(shape, lambda *_: (0,) * len(shape), pipeline_mode=pl.Buffered(1))


def _bias_a_table_index():
    u = np.arange(2 * A_QBLOCK)
    key_minus_query = np.where(u <= A_QBLOCK, u, u - 2 * A_QBLOCK)
    rel = np.arange(A_KEY_TILES)[:, None] * A_QBLOCK - key_minus_query[None, :]
    return np.clip(rel, -(CHUNK - 1), MAX_REL) + (CHUNK - 1)


def _bias_b_table():
    slopes = np.array([2.0 ** (-8.0 * (h + 1) / B_Q_HEADS) for h in range(B_Q_HEADS)], np.float32)
    i = np.arange(B_QBLOCK)[:, None]
    j = np.arange(B_WINDOW)[None, :]
    rel = i - j + B_QBLOCK
    chunk_diff = i // CHUNK - j // CHUNK + B_QBLOCK // CHUNK
    vis = (chunk_diff >= 0) & (chunk_diff <= B_PREV_CHUNKS)
    dist = np.abs(rel).astype(np.float32)
    out = np.empty((B_KV_HEADS, B_GROUP * B_QBLOCK, B_WINDOW), np.float32)
    for g in range(B_KV_HEADS):
        for c in range(B_GROUP):
            out[g, c * B_QBLOCK:(c + 1) * B_QBLOCK] = np.where(
                vis, -slopes[g * B_GROUP + c] * dist, NEG_INF)
    return out


def kernel(x, ffn1_norm, ffn1_w_gate, ffn1_w_up, ffn1_w_down, mix_norm, w_in, rel_bias, sinks,
           w_proj_a, w_proj_b, w_out, ffn2_norm, ffn2_w_gate, ffn2_w_up, ffn2_w_down, final_norm):
    batch, seq, d = x.shape
    rows = batch * seq
    assert d == D_MODEL and rows % ROW_TILE == 0 and seq % A_QBLOCK == 0
    n_row_tiles = rows // ROW_TILE
    scale = 1.0 / np.sqrt(D_HEAD)

    b_perm = np.concatenate([np.arange(h * D_HEAD, (h + 1) * D_HEAD)
                             for c in range(B_GROUP) for h in (c, B_GROUP + c)])
    cuts = np.cumsum([A_WIDTH, A_WIDTH, A_WIDTH, B_Q_WIDTH, B_KV_WIDTH, B_KV_WIDTH, D_MODEL])
    wqa, wka, wva, wqb, wkb, wvb, wga, wgb = jnp.split(w_in, cuts, axis=1)
    w_qkva = jnp.concatenate([wqa * scale, wka, wva], axis=1).astype(BF16)
    w_qkvb = jnp.concatenate([wqb[:, b_perm] * scale, wkb, wvb], axis=1).astype(BF16)
    w_gates = jnp.concatenate([wga, wgb], axis=1).astype(BF16)
    w_pb = w_proj_b[b_perm, :].astype(BF16)
    vec = lambda g: g.reshape(1, D_MODEL).astype(F32)

    row_block = lambda width: pl.BlockSpec((ROW_TILE, width), lambda i: (i, 0))
    dense_params = pltpu.CompilerParams(dimension_semantics=("arbitrary",),
                                        vmem_limit_bytes=VMEM_LIMIT_BYTES)

    x1, qkva, qkvb, gates = pl.pallas_call(
        _ffn_in_kernel,
        grid=(n_row_tiles,),
        in_specs=[row_block(D_MODEL), _resident((1, D_MODEL)),
                  _resident((D_MODEL, D_FF)), _resident((D_MODEL, D_FF)),
                  _resident((D_FF, D_MODEL)), _resident((1, D_MODEL)),
                  _resident((D_MODEL, QKV_A_WIDTH)), _resident((D_MODEL, QKV_B_WIDTH)),
                  _resident((D_MODEL, GATES_WIDTH))],
        out_specs=[row_block(D_MODEL), row_block(QKV_A_WIDTH), row_block(QKV_B_WIDTH),
                   row_block(GATES_WIDTH)],
        out_shape=[jax.ShapeDtypeStruct((rows, D_MODEL), F32),
                   jax.ShapeDtypeStruct((rows, QKV_A_WIDTH), BF16),
                   jax.ShapeDtypeStruct((rows, QKV_B_WIDTH), BF16),
                   jax.ShapeDtypeStruct((rows, GATES_WIDTH), BF16)],
        scratch_shapes=[pltpu.VMEM((ROW_TILE, D_FF), BF16)],
        compiler_params=dense_params,
        name="ffn_in",
    )(x.reshape(rows, D_MODEL), vec(ffn1_norm), ffn1_w_gate.astype(BF16),
      ffn1_w_up.astype(BF16), ffn1_w_down.astype(BF16), vec(mix_norm), w_qkva, w_qkvb, w_gates)

    tab_a = rel_bias.astype(F32)[:, _bias_a_table_index()][:, :, None, :]
    n_pairs = A_WIDTH // LANES
    seq_block = lambda col: pl.BlockSpec((None, seq, LANES), lambda j, b, col=col: (b, 0, col + j))
    ya = pl.pallas_call(
        _attn_a_kernel,
        grid=(n_pairs, batch),
        in_specs=[seq_block(0), seq_block(n_pairs), seq_block(2 * n_pairs),
                  pl.BlockSpec((2, A_KEY_TILES, 1, 2 * A_QBLOCK), lambda j, b: (j, 0, 0, 0))],
        out_specs=seq_block(0),
        out_shape=jax.ShapeDtypeStruct((batch, seq, A_WIDTH), BF16),
        scratch_shapes=[pltpu.VMEM((2, A_KEY_TILES, A_QBLOCK, A_QBLOCK), F32)],
        compiler_params=pltpu.CompilerParams(dimension_semantics=("arbitrary", "arbitrary"),
                                             vmem_limit_bytes=VMEM_LIMIT_BYTES),
        name="attn_a",
    )(*([qkva.reshape(batch, seq, QKV_A_WIDTH)] * 3), tab_a)

    qkvb3 = qkvb.reshape(batch, seq, QKV_B_WIDTH)
    kv_col = B_Q_WIDTH // LANES
    yb = pl.pallas_call(
        _attn_b_kernel,
        grid=(batch,),
        in_specs=[pl.BlockSpec(memory_space=pltpu.SMEM),
                  pl.BlockSpec((None, seq, B_Q_WIDTH), lambda b: (b, 0, 0)),
                  pl.BlockSpec((None, seq, LANES), lambda b: (b, 0, kv_col)),
                  pl.BlockSpec((None, seq, LANES), lambda b: (b, 0, kv_col + 1)),
                  _resident((B_KV_HEADS, B_GROUP * B_QBLOCK, B_WINDOW))],
        out_specs=pl.BlockSpec((None, seq, B_Q_WIDTH), lambda b: (b, 0, 0)),
        out_shape=jax.ShapeDtypeStruct((batch, seq, B_Q_WIDTH), BF16),
        compiler_params=pltpu.CompilerParams(dimension_semantics=("arbitrary",),
                                             vmem_limit_bytes=VMEM_LIMIT_BYTES),
        name="attn_b",
    )(sinks.astype(F32), qkvb3, qkvb3, qkvb3, jnp.asarray(_bias_b_table()))

    gate_block = lambda col: pl.BlockSpec((ROW_TILE, D_MODEL), lambda i, col=col: (i, col))
    out = pl.pallas_call(
        _ffn_out_kernel,
        grid=(n_row_tiles,),
        in_specs=[row_block(D_MODEL), row_block(A_WIDTH), row_block(B_Q_WIDTH),
                  gate_block(0), gate_block(1),
                  _resident((A_WIDTH, D_MODEL)), _resident((B_Q_WIDTH, D_MODEL)),
                  _resident((D_MODEL, D_MODEL)), _resident((1, D_MODEL)),
                  _resident((D_MODEL, D_FF)), _resident((D_MODEL, D_FF)),
                  _resident((D_FF, D_MODEL)), _resident((1, D_MODEL))],
        out_specs=row_block(D_MODEL),
        out_shape=jax.ShapeDtypeStruct((rows, D_MODEL), F32),
        scratch_shapes=[pltpu.VMEM((ROW_TILE, D_FF), BF16)],
        compiler_params=dense_params,
        name="ffn_out",
    )(x1, ya.reshape(rows, A_WIDTH), yb.reshape(rows, B_Q_WIDTH), gates, gates,
      w_proj_a.astype(BF16), w_pb, w_out.astype(BF16), vec(ffn2_norm),
      ffn2_w_gate.astype(BF16), ffn2_w_up.astype(BF16), ffn2_w_down.astype(BF16),
      vec(final_norm))
    return out.reshape(batch, seq, D_MODEL)
```

```python
import functools

import numpy as np
import jax
import jax.numpy as jnp
from jax import lax
from jax.experimental import pallas as pl
from jax.experimental.pallas import tpu as pltpu

F32 = jnp.float32
BF16 = jnp.bfloat16

D_MODEL = 1024
D_FF = 2816
CHUNK = 64
D_HEAD = 64
A_HEADS = 8
A_PREV_CHUNKS = 8
MAX_REL = 128
B_Q_HEADS = 8
B_KV_HEADS = 2
B_GROUP = B_Q_HEADS // B_KV_HEADS
B_PREV_CHUNKS = 2
A_WIDTH = A_HEADS * D_HEAD
B_Q_WIDTH = B_Q_HEADS * D_HEAD
B_KV_WIDTH = B_KV_HEADS * D_HEAD
QKV_A_WIDTH = 3 * A_WIDTH
QKV_B_WIDTH = B_Q_WIDTH + 2 * B_KV_WIDTH
GATES_WIDTH = 2 * D_MODEL
EPS = 1e-6
NEG_INF = -1e30

LANES = 128
MXU_DIM = 256
VMEM_LIMIT_BYTES = 56 * 1024 * 1024

ROW_TILE = 512
FF_CHUNK = MXU_DIM
A_QBLOCK = 4 * CHUNK
A_KEY_TILES = A_PREV_CHUNKS * CHUNK // A_QBLOCK + 1
B_QBLOCK = 2 * CHUNK
B_WINDOW = 2 * B_QBLOCK


def _rms_norm(x, gain):
    return x * lax.rsqrt(jnp.mean(x * x, axis=-1, keepdims=True) + EPS) * gain


def _swiglu(h, wg_ref, wu_ref, wd_ref, hid_ref):
    for c in range(D_FF // FF_CHUNK):
        cols = slice(c * FF_CHUNK, (c + 1) * FF_CHUNK)
        g = jnp.dot(h, wg_ref[:, cols], preferred_element_type=F32)
        u = jnp.dot(h, wu_ref[:, cols], preferred_element_type=F32)
        hid_ref[:, cols] = (g * jax.nn.sigmoid(g) * u).astype(BF16)
    return jnp.dot(hid_ref[...], wd_ref[...], preferred_element_type=F32)


def _ffn_in_kernel(x_ref, g1_ref, wg_ref, wu_ref, wd_ref, gm_ref, wa_ref, wb_ref, wgt_ref,
                   x1_ref, qkva_ref, qkvb_ref, gates_ref, hid_ref):
    x = x_ref[...]
    h = _rms_norm(x, g1_ref[...]).astype(BF16)
    x1 = x + 0.5 * _swiglu(h, wg_ref, wu_ref, wd_ref, hid_ref)
    x1_ref[...] = x1
    h2 = _rms_norm(x1, gm_ref[...]).astype(BF16)
    qkva_ref[...] = jnp.dot(h2, wa_ref[...], preferred_element_type=F32).astype(BF16)
    qkvb_ref[...] = jnp.dot(h2, wb_ref[...], preferred_element_type=F32).astype(BF16)
    gates_ref[...] = jnp.dot(h2, wgt_ref[...], preferred_element_type=F32).astype(BF16)


def _ffn_out_kernel(x1_ref, ya_ref, yb_ref, ga_ref, gb_ref, wpa_ref, wpb_ref, wo_ref,
                    g2_ref, wg_ref, wu_ref, wd_ref, gf_ref, out_ref, hid_ref):
    pa = jnp.dot(ya_ref[...], wpa_ref[...], preferred_element_type=F32)
    pb = jnp.dot(yb_ref[...], wpb_ref[...], preferred_element_type=F32)
    merged = (jax.nn.sigmoid(ga_ref[...].astype(F32)) * pa
              + jax.nn.sigmoid(gb_ref[...].astype(F32)) * pb)
    x2 = x1_ref[...] + jnp.dot(merged.astype(BF16), wo_ref[...], preferred_element_type=F32)
    h = _rms_norm(x2, g2_ref[...]).astype(BF16)
    x3 = x2 + 0.5 * _swiglu(h, wg_ref, wu_ref, wd_ref, hid_ref)
    out_ref[...] = _rms_norm(x3, gf_ref[...])


def _dot_nt(a, b):
    return lax.dot_general(a, b, (((1,), (1,)), ((), ())), preferred_element_type=F32)


def _attn_a_kernel(q_ref, k_ref, v_ref, tab_ref, o_ref, bias_ref, vext_ref):
    seq = q_ref.shape[0]
    low = lax.broadcasted_iota(jnp.int32, (1, LANES), 1) < D_HEAD

    @pl.when(pl.program_id(1) == 0)
    def _build_bias():
        qc = lax.broadcasted_iota(jnp.int32, (A_QBLOCK, A_QBLOCK), 0) // CHUNK
        kc = lax.broadcasted_iota(jnp.int32, (A_QBLOCK, A_QBLOCK), 1) // CHUNK
        for hh in range(2):
            for d in range(A_KEY_TILES):
                row = jnp.broadcast_to(tab_ref[hh, d], (A_QBLOCK, 2 * A_QBLOCK))
                toeplitz = pltpu.roll(row, 0, 1, stride=1, stride_axis=0)[:, :A_QBLOCK]
                chunk_diff = d * (A_QBLOCK // CHUNK) + qc - kc
                visible = (chunk_diff >= 0) & (chunk_diff <= A_PREV_CHUNKS)
                bias_ref[d, hh * A_QBLOCK:(hh + 1) * A_QBLOCK, :] = jnp.where(
                    visible, toeplitz, NEG_INF)

    vext_ref[:, :LANES] = v_ref[...]
    vext_ref[:, LANES:] = jnp.ones((seq, LANES), BF16)

    for m in range(seq // A_QBLOCK):
        q = q_ref[m * A_QBLOCK:(m + 1) * A_QBLOCK, :]
        zero = jnp.zeros_like(q)
        qs = jnp.concatenate([jnp.where(low, q, zero), jnp.where(low, zero, q)], axis=0)
        dists = [d for d in range(A_KEY_TILES - 1, -1, -1) if m - d >= 0]
        key_rows = [slice((m - d) * A_QBLOCK, (m - d + 1) * A_QBLOCK) for d in dists]
        s = [_dot_nt(qs, k_ref[rows, :]) + bias_ref[d] for d, rows in zip(dists, key_rows)]
        mx = jnp.max(functools.reduce(jnp.maximum, s), axis=-1, keepdims=True)
        acc = None
        for t, rows in zip(s, key_rows):
            pv = jnp.dot(jnp.exp(t - mx).astype(BF16), vext_ref[rows, :],
                         preferred_element_type=F32)
            acc = pv if acc is None else acc + pv
        o = acc[:, :LANES] / acc[:, LANES:]
        o_ref[m * A_QBLOCK:(m + 1) * A_QBLOCK, :] = jnp.where(
            low, o[:A_QBLOCK], o[A_QBLOCK:]).astype(BF16)


def _attn_b_kernel(sinks_ref, q_ref, k_ref, v_ref, bias_ref, o_ref, vext_ref):
    seq = q_ref.shape[0]
    low = lax.broadcasted_iota(jnp.int32, (1, LANES), 1) < D_HEAD
    k_all = k_ref[...]
    k_half = [jnp.where(low if g == 0 else ~low, k_all, jnp.zeros_like(k_all))
              for g in range(B_KV_HEADS)]
    sink = [jnp.concatenate([jnp.full((B_QBLOCK, LANES), sinks_ref[g * B_GROUP + c], F32)
                             for c in range(B_GROUP)], axis=0) for g in range(B_KV_HEADS)]
    vext_ref[:, :LANES] = v_ref[...]
    vext_ref[:, LANES:] = jnp.ones((seq, LANES), BF16)
    for m in range(seq // B_QBLOCK):
        qrows = slice(m * B_QBLOCK, (m + 1) * B_QBLOCK)
        krows = slice(max(m - 1, 0) * B_QBLOCK, (m + 1) * B_QBLOCK)
        bcols = slice(B_WINDOW - (krows.stop - krows.start), B_WINDOW)
        qs = jnp.concatenate([q_ref[qrows, c * LANES:(c + 1) * LANES] for c in range(B_GROUP)],
                             axis=0)
        outs = []
        for g in range(B_KV_HEADS):
            s = _dot_nt(qs, k_half[g][krows, :]) + bias_ref[g, :, bcols]
            mx = jnp.maximum(jnp.max(s, axis=-1, keepdims=True), sink[g])
            p = jnp.exp(s - jnp.concatenate([mx] * (s.shape[1] // LANES), axis=1))
            pv = jnp.dot(p.astype(BF16), vext_ref[krows, :], preferred_element_type=F32)
            outs.append(pv[:, :LANES] / (pv[:, LANES:] + jnp.exp(sink[g] - mx)))
        merged = jnp.where(low, outs[0], outs[1]).astype(BF16)
        for c in range(B_GROUP):
            o_ref[qrows, c * LANES:(c + 1) * LANES] = merged[c * B_QBLOCK:(c + 1) * B_QBLOCK, :]


def _resident(shape):
    return pl.BlockSpec(shape, lambda *_: (0,) * len(shape), pipeline_mode=pl.Buffered(1))


def _bias_a_table_index():
    u = np.arange(2 * A_QBLOCK)
    key_minus_query = np.where(u <= A_QBLOCK, u, u - 2 * A_QBLOCK)
    rel = np.arange(A_KEY_TILES)[:, None] * A_QBLOCK - key_minus_query[None, :]
    return np.clip(rel, -(CHUNK - 1), MAX_REL) + (CHUNK - 1)


def _bias_b_table():
    slopes = np.array([2.0 ** (-8.0 * (h + 1) / B_Q_HEADS) for h in range(B_Q_HEADS)], np.float32)
    i = np.arange(B_QBLOCK)[:, None]
    j = np.arange(B_WINDOW)[None, :]
    rel = i - j + B_QBLOCK
    chunk_diff = i // CHUNK - j // CHUNK + B_QBLOCK // CHUNK
    vis = (chunk_diff >= 0) & (chunk_diff <= B_PREV_CHUNKS)
    dist = np.abs(rel).astype(np.float32)
    out = np.empty((B_KV_HEADS, B_GROUP * B_QBLOCK, B_WINDOW), np.float32)
    for g in range(B_KV_HEADS):
        for c in range(B_GROUP):
            out[g, c * B_QBLOCK:(c + 1) * B_QBLOCK] = np.where(
                vis, -slopes[g * B_GROUP + c] * dist, NEG_INF)
    return out


def kernel(x, ffn1_norm, ffn1_w_gate, ffn1_w_up, ffn1_w_down, mix_norm, w_in, rel_bias, sinks,
           w_proj_a, w_proj_b, w_out, ffn2_norm, ffn2_w_gate, ffn2_w_up, ffn2_w_down, final_norm):
    batch, seq, d = x.shape
    rows = batch * seq
    assert d == D_MODEL and rows % ROW_TILE == 0 and seq % A_QBLOCK == 0
    n_row_tiles = rows // ROW_TILE
    scale = 1.0 / np.sqrt(D_HEAD)

    b_perm = np.concatenate([np.arange(h * D_HEAD, (h + 1) * D_HEAD)
                             for c in range(B_GROUP) for h in (c, B_GROUP + c)])
    cuts = np.cumsum([A_WIDTH, A_WIDTH, A_WIDTH, B_Q_WIDTH, B_KV_WIDTH, B_KV_WIDTH, D_MODEL])
    wqa, wka, wva, wqb, wkb, wvb, wga, wgb = jnp.split(w_in, cuts, axis=1)
    w_qkva = jnp.concatenate([wqa * scale, wka, wva], axis=1).astype(BF16)
    w_qkvb = jnp.concatenate([wqb[:, b_perm] * scale, wkb, wvb], axis=1).astype(BF16)
    w_gates = jnp.concatenate([wga, wgb], axis=1).astype(BF16)
    w_pb = w_proj_b[b_perm, :].astype(BF16)
    vec = lambda g: g.reshape(1, D_MODEL).astype(F32)

    row_block = lambda width: pl.BlockSpec((ROW_TILE, width), lambda i: (i, 0))
    dense_params = pltpu.CompilerParams(dimension_semantics=("arbitrary",),
                                        vmem_limit_bytes=VMEM_LIMIT_BYTES)

    x1, qkva, qkvb, gates = pl.pallas_call(
        _ffn_in_kernel,
        grid=(n_row_tiles,),
        in_specs=[row_block(D_MODEL), _resident((1, D_MODEL)),
                  _resident((D_MODEL, D_FF)), _resident((D_MODEL, D_FF)),
                  _resident((D_FF, D_MODEL)), _resident((1, D_MODEL)),
                  _resident((D_MODEL, QKV_A_WIDTH)), _resident((D_MODEL, QKV_B_WIDTH)),
                  _resident((D_MODEL, GATES_WIDTH))],
        out_specs=[row_block(D_MODEL), row_block(QKV_A_WIDTH), row_block(QKV_B_WIDTH),
                   row_block(GATES_WIDTH)],
        out_shape=[jax.ShapeDtypeStruct((rows, D_MODEL), F32),
                   jax.ShapeDtypeStruct((rows, QKV_A_WIDTH), BF16),
                   jax.ShapeDtypeStruct((rows, QKV_B_WIDTH), BF16),
                   jax.ShapeDtypeStruct((rows, GATES_WIDTH), BF16)],
        scratch_shapes=[pltpu.VMEM((ROW_TILE, D_FF), BF16)],
        compiler_params=dense_params,
        name="ffn_in",
    )(x.reshape(rows, D_MODEL), vec(ffn1_norm), ffn1_w_gate.astype(BF16),
      ffn1_w_up.astype(BF16), ffn1_w_down.astype(BF16), vec(mix_norm), w_qkva, w_qkvb, w_gates)

    tab_a = rel_bias.astype(F32)[:, _bias_a_table_index()][:, :, None, :]
    n_pairs = A_WIDTH // LANES
    seq_block = lambda col: pl.BlockSpec((None, seq, LANES), lambda j, b, col=col: (b, 0, col + j))
    ya = pl.pallas_call(
        _attn_a_kernel,
        grid=(n_pairs, batch),
        in_specs=[seq_block(0), seq_block(n_pairs), seq_block(2 * n_pairs),
                  pl.BlockSpec((2, A_KEY_TILES, 1, 2 * A_QBLOCK), lambda j, b: (j, 0, 0, 0))],
        out_specs=seq_block(0),
        out_shape=jax.ShapeDtypeStruct((batch, seq, A_WIDTH), BF16),
        scratch_shapes=[pltpu.VMEM((A_KEY_TILES, 2 * A_QBLOCK, A_QBLOCK), F32),
                        pltpu.VMEM((seq, 2 * LANES), BF16)],
        compiler_params=pltpu.CompilerParams(dimension_semantics=("arbitrary", "arbitrary"),
                                             vmem_limit_bytes=VMEM_LIMIT_BYTES),
        name="attn_a",
    )(*([qkva.reshape(batch, seq, QKV_A_WIDTH)] * 3), tab_a)

    qkvb3 = qkvb.reshape(batch, seq, QKV_B_WIDTH)
    kv_col = B_Q_WIDTH // LANES
    yb = pl.pallas_call(
        _attn_b_kernel,
        grid=(batch,),
        in_specs=[pl.BlockSpec(memory_space=pltpu.SMEM),
                  pl.BlockSpec((None, seq, B_Q_WIDTH), lambda b: (b, 0, 0)),
                  pl.BlockSpec((None, seq, LANES), lambda b: (b, 0, kv_col)),
                  pl.BlockSpec((None, seq, LANES), lambda b: (b, 0, kv_col + 1)),
                  _resident((B_KV_HEADS, B_GROUP * B_QBLOCK, B_WINDOW))],
        out_specs=pl.BlockSpec((None, seq, B_Q_WIDTH), lambda b: (b, 0, 0)),
        out_shape=jax.ShapeDtypeStruct((batch, seq, B_Q_WIDTH), BF16),
        scratch_shapes=[pltpu.VMEM((seq, 2 * LANES), BF16)],
        compiler_params=pltpu.CompilerParams(dimension_semantics=("arbitrary",),
                                             vmem_limit_bytes=VMEM_LIMIT_BYTES),
        name="attn_b",
    )(sinks.astype(F32), qkvb3, qkvb3, qkvb3, jnp.asarray(_bias_b_table()))

    gate_block = lambda col: pl.BlockSpec((ROW_TILE, D_MODEL), lambda i, col=col: (i, col))
    out = pl.pallas_call(
        _ffn_out_kernel,
        grid=(n_row_tiles,),
        in_specs=[row_block(D_MODEL), row_block(A_WIDTH), row_block(B_Q_WIDTH),
                  gate_block(0), gate_block(1),
                  _resident((A_WIDTH, D_MODEL)), _resident((B_Q_WIDTH, D_MODEL)),
                  _resident((D_MODEL, D_MODEL)), _resident((1, D_MODEL)),
                  _resident((D_MODEL, D_FF)), _resident((D_MODEL, D_FF)),
                  _resident((D_FF, D_MODEL)), _resident((1, D_MODEL))],
        out_specs=row_block(D_MODEL),
        out_shape=jax.ShapeDtypeStruct((rows, D_MODEL), F32),
        scratch_shapes=[pltpu.VMEM((ROW_TILE, D_FF), BF16)],
        compiler_params=dense_params,
        name="ffn_out",
    )(x1, ya.reshape(rows, A_WIDTH), yb.reshape(rows, B_Q_WIDTH), gates, gates,
      w_proj_a.astype(BF16), w_pb, w_out.astype(BF16), vec(ffn2_norm),
      ffn2_w_gate.astype(BF16), ffn2_w_up.astype(BF16), ffn2_w_down.astype(BF16),
      vec(final_norm))
    return out.reshape(batch, seq, D_MODEL)
```

```python
import functools

import numpy as np
import jax
import jax.numpy as jnp
from jax import lax
from jax.experimental import pallas as pl
from jax.experimental.pallas import tpu as pltpu

F32 = jnp.float32
BF16 = jnp.bfloat16

D_MODEL = 1024
D_FF = 2816
CHUNK = 64
D_HEAD = 64
A_HEADS = 8
A_PREV_CHUNKS = 8
MAX_REL = 128
B_Q_HEADS = 8
B_KV_HEADS = 2
B_GROUP = B_Q_HEADS // B_KV_HEADS
B_PREV_CHUNKS = 2
A_WIDTH = A_HEADS * D_HEAD
B_Q_WIDTH = B_Q_HEADS * D_HEAD
B_KV_WIDTH = B_KV_HEADS * D_HEAD
QKV_A_WIDTH = 3 * A_WIDTH
QKV_B_WIDTH = B_Q_WIDTH + 2 * B_KV_WIDTH
GATES_WIDTH = 2 * D_MODEL
EPS = 1e-6
NEG_INF = -1e30

LANES = 128
MXU_DIM = 256
VMEM_LIMIT_BYTES = 56 * 1024 * 1024

ROW_TILE = 512
FF_CHUNK = MXU_DIM
A_QBLOCK = 4 * CHUNK
A_KEY_TILES = A_PREV_CHUNKS * CHUNK // A_QBLOCK + 1
B_QBLOCK = 2 * CHUNK
B_WINDOW = 2 * B_QBLOCK


def _rms_norm(x, gain):
    return x * lax.rsqrt(jnp.mean(x * x, axis=-1, keepdims=True) + EPS) * gain


def _swiglu(h, wg_ref, wu_ref, wd_ref, hid_ref):
    for c in range(D_FF // FF_CHUNK):
        cols = slice(c * FF_CHUNK, (c + 1) * FF_CHUNK)
        g = jnp.dot(h, wg_ref[:, cols], preferred_element_type=F32)
        u = jnp.dot(h, wu_ref[:, cols], preferred_element_type=F32)
        hid_ref[:, cols] = (g * jax.nn.sigmoid(g) * u).astype(BF16)
    return jnp.dot(hid_ref[...], wd_ref[...], preferred_element_type=F32)


def _ffn_in_kernel(x_ref, g1_ref, wg_ref, wu_ref, wd_ref, gm_ref, wa_ref, wb_ref, wgt_ref,
                   x1_ref, qkva_ref, qkvb_ref, gates_ref, hid_ref):
    x = x_ref[...]
    h = _rms_norm(x, g1_ref[...]).astype(BF16)
    x1 = x + 0.5 * _swiglu(h, wg_ref, wu_ref, wd_ref, hid_ref)
    x1_ref[...] = x1
    h2 = _rms_norm(x1, gm_ref[...]).astype(BF16)
    qkva_ref[...] = jnp.dot(h2, wa_ref[...], preferred_element_type=F32).astype(BF16)
    qkvb_ref[...] = jnp.dot(h2, wb_ref[...], preferred_element_type=F32).astype(BF16)
    gates_ref[...] = jnp.dot(h2, wgt_ref[...], preferred_element_type=F32).astype(BF16)


def _ffn_out_kernel(x1_ref, ya_ref, yb_ref, ga_ref, gb_ref, wpa_ref, wpb_ref, wo_ref,
                    g2_ref, wg_ref, wu_ref, wd_ref, gf_ref, out_ref, hid_ref):
    pa = jnp.dot(ya_ref[...], wpa_ref[...], preferred_element_type=F32)
    pb = jnp.dot(yb_ref[...], wpb_ref[...], preferred_element_type=F32)
    merged = (jax.nn.sigmoid(ga_ref[...].astype(F32)) * pa
              + jax.nn.sigmoid(gb_ref[...].astype(F32)) * pb)
    x2 = x1_ref[...] + jnp.dot(merged.astype(BF16), wo_ref[...], preferred_element_type=F32)
    h = _rms_norm(x2, g2_ref[...]).astype(BF16)
    x3 = x2 + 0.5 * _swiglu(h, wg_ref, wu_ref, wd_ref, hid_ref)
    out_ref[...] = _rms_norm(x3, gf_ref[...])


def _dot_nt(a, b):
    return lax.dot_general(a, b, (((1,), (1,)), ((), ())), preferred_element_type=F32)


def _attn_kernel(sinks_ref, qkva_ref, qkvb_ref, tab_ref, bias_b_ref, ya_ref, yb_ref,
                 bias_a_ref, vext_a_ref, vext_b_ref):
    seq = qkva_ref.shape[0]
    n_pairs = A_WIDTH // LANES
    low = lax.broadcasted_iota(jnp.int32, (1, LANES), 1) < D_HEAD
    ones = jnp.ones((seq, LANES), BF16)

    @pl.when(pl.program_id(0) == 0)
    def _build_bias_a():
        qc = lax.broadcasted_iota(jnp.int32, (A_QBLOCK, A_QBLOCK), 0) // CHUNK
        kc = lax.broadcasted_iota(jnp.int32, (A_QBLOCK, A_QBLOCK), 1) // CHUNK
        for h in range(A_HEADS):
            for d in range(A_KEY_TILES):
                row = jnp.broadcast_to(tab_ref[h, d], (A_QBLOCK, 2 * A_QBLOCK))
                toeplitz = pltpu.roll(row, 0, 1, stride=1, stride_axis=0)[:, :A_QBLOCK]
                chunk_diff = d * (A_QBLOCK // CHUNK) + qc - kc
                visible = (chunk_diff >= 0) & (chunk_diff <= A_PREV_CHUNKS)
                bias_a_ref[h // 2, d, (h % 2) * A_QBLOCK:(h % 2 + 1) * A_QBLOCK, :] = jnp.where(
                    visible, toeplitz, NEG_INF)

    def mixer_a_block(j, m):
        qrows = slice(m * A_QBLOCK, (m + 1) * A_QBLOCK)
        q = qkva_ref[qrows, j * LANES:(j + 1) * LANES]
        zero = jnp.zeros_like(q)
        qs = jnp.concatenate([jnp.where(low, q, zero), jnp.where(low, zero, q)], axis=0)
        dists = [d for d in range(A_KEY_TILES - 1, -1, -1) if m - d >= 0]
        key_rows = [slice((m - d) * A_QBLOCK, (m - d + 1) * A_QBLOCK) for d in dists]
        kcols = slice(A_WIDTH + j * LANES, A_WIDTH + (j + 1) * LANES)
        s = [_dot_nt(qs, qkva_ref[rows, kcols]) + bias_a_ref[j, d]
             for d, rows in zip(dists, key_rows)]
        mx = jnp.max(functools.reduce(jnp.maximum, s), axis=-1, keepdims=True)
        acc = None
        for t, rows in zip(s, key_rows):
            pv = jnp.dot(jnp.exp(t - mx).astype(BF16),
                         vext_a_ref[rows, 2 * j * LANES:2 * (j + 1) * LANES],
                         preferred_element_type=F32)
            acc = pv if acc is None else acc + pv
        o = acc[:, :LANES] / acc[:, LANES:]
        ya_ref[qrows, j * LANES:(j + 1) * LANES] = jnp.where(
            low, o[:A_QBLOCK], o[A_QBLOCK:]).astype(BF16)

    def mixer_b_block(m):
        qrows = slice(m * B_QBLOCK, (m + 1) * B_QBLOCK)
        krows = slice(max(m - 1, 0) * B_QBLOCK, (m + 1) * B_QBLOCK)
        bcols = slice(B_WINDOW - (krows.stop - krows.start), B_WINDOW)
        qs = jnp.concatenate([qkvb_ref[qrows, c * LANES:(c + 1) * LANES] for c in range(B_GROUP)],
                             axis=0)
        k = qkvb_ref[krows, B_Q_WIDTH:B_Q_WIDTH + LANES]
        outs = []
        for g in range(B_KV_HEADS):
            kg = jnp.where(low if g == 0 else ~low, k, jnp.zeros_like(k))
            s = _dot_nt(qs, kg) + bias_b_ref[g, :, bcols]
            mx = jnp.maximum(jnp.max(s, axis=-1, keepdims=True), sink[g])
            p = jnp.exp(s - jnp.concatenate([mx] * (s.shape[1] // LANES), axis=1))
            pv = jnp.dot(p.astype(BF16), vext_b_ref[krows, :], preferred_element_type=F32)
            outs.append(pv[:, :LANES] / (pv[:, LANES:] + jnp.exp(sink[g] - mx)))
        merged = jnp.where(low, outs[0], outs[1]).astype(BF16)
        for c in range(B_GROUP):
            yb_ref[qrows, c * LANES:(c + 1) * LANES] = merged[c * B_QBLOCK:(c + 1) * B_QBLOCK, :]

    for j in range(n_pairs):
        vcols = slice(2 * A_WIDTH + j * LANES, 2 * A_WIDTH + (j + 1) * LANES)
        vext_a_ref[:, 2 * j * LANES:(2 * j + 1) * LANES] = qkva_ref[:, vcols]
        vext_a_ref[:, (2 * j + 1) * LANES:2 * (j + 1) * LANES] = ones
    vext_b_ref[:, :LANES] = qkvb_ref[:, B_Q_WIDTH + LANES:]
    vext_b_ref[:, LANES:] = ones
    sink = [jnp.concatenate([jnp.full((B_QBLOCK, LANES), sinks_ref[g * B_GROUP + c], F32)
                             for c in range(B_GROUP)], axis=0) for g in range(B_KV_HEADS)]

    b_blocks_per_pair = seq // B_QBLOCK // n_pairs
    for j in range(n_pairs):
        for m in range(seq // A_QBLOCK):
            mixer_a_block(j, m)
        for m in range(j * b_blocks_per_pair, (j + 1) * b_blocks_per_pair):
            mixer_b_block(m)


def _resident(shape):
    return pl.BlockSpec(shape, lambda *_: (0,) * len(shape), pipeline_mode=pl.Buffered(1))


def _bias_a_table_index():
    u = np.arange(2 * A_QBLOCK)
    key_minus_query = np.where(u <= A_QBLOCK, u, u - 2 * A_QBLOCK)
    rel = np.arange(A_KEY_TILES)[:, None] * A_QBLOCK - key_minus_query[None, :]
    return np.clip(rel, -(CHUNK - 1), MAX_REL) + (CHUNK - 1)


def _bias_b_table():
    slopes = np.array([2.0 ** (-8.0 * (h + 1) / B_Q_HEADS) for h in range(B_Q_HEADS)], np.float32)
    i = np.arange(B_QBLOCK)[:, None]
    j = np.arange(B_WINDOW)[None, :]
    rel = i - j + B_QBLOCK
    chunk_diff = i // CHUNK - j // CHUNK + B_QBLOCK // CHUNK
    vis = (chunk_diff >= 0) & (chunk_diff <= B_PREV_CHUNKS)
    dist = np.abs(rel).astype(np.float32)
    out = np.empty((B_KV_HEADS, B_GROUP * B_QBLOCK, B_WINDOW), np.float32)
    for g in range(B_KV_HEADS):
        for c in range(B_GROUP):
            out[g, c * B_QBLOCK:(c + 1) * B_QBLOCK] = np.where(
                vis, -slopes[g * B_GROUP + c] * dist, NEG_INF)
    return out


def kernel(x, ffn1_norm, ffn1_w_gate, ffn1_w_up, ffn1_w_down, mix_norm, w_in, rel_bias, sinks,
           w_proj_a, w_proj_b, w_out, ffn2_norm, ffn2_w_gate, ffn2_w_up, ffn2_w_down, final_norm):
    batch, seq, d = x.shape
    rows = batch * seq
    assert d == D_MODEL and rows % ROW_TILE == 0 and seq % A_QBLOCK == 0
    n_row_tiles = rows // ROW_TILE
    scale = 1.0 / np.sqrt(D_HEAD)

    b_perm = np.concatenate([np.arange(h * D_HEAD, (h + 1) * D_HEAD)
                             for c in range(B_GROUP) for h in (c, B_GROUP + c)])
    cuts = np.cumsum([A_WIDTH, A_WIDTH, A_WIDTH, B_Q_WIDTH, B_KV_WIDTH, B_KV_WIDTH, D_MODEL])
    wqa, wka, wva, wqb, wkb, wvb, wga, wgb = jnp.split(w_in, cuts, axis=1)
    w_qkva = jnp.concatenate([wqa * scale, wka, wva], axis=1).astype(BF16)
    w_qkvb = jnp.concatenate([wqb[:, b_perm] * scale, wkb, wvb], axis=1).astype(BF16)
    w_gates = jnp.concatenate([wga, wgb], axis=1).astype(BF16)
    w_pb = w_proj_b[b_perm, :].astype(BF16)
    vec = lambda g: g.reshape(1, D_MODEL).astype(F32)

    row_block = lambda width: pl.BlockSpec((ROW_TILE, width), lambda i: (i, 0))
    dense_params = pltpu.CompilerParams(dimension_semantics=("arbitrary",),
                                        vmem_limit_bytes=VMEM_LIMIT_BYTES)

    x1, qkva, qkvb, gates = pl.pallas_call(
        _ffn_in_kernel,
        grid=(n_row_tiles,),
        in_specs=[row_block(D_MODEL), _resident((1, D_MODEL)),
                  _resident((D_MODEL, D_FF)), _resident((D_MODEL, D_FF)),
                  _resident((D_FF, D_MODEL)), _resident((1, D_MODEL)),
                  _resident((D_MODEL, QKV_A_WIDTH)), _resident((D_MODEL, QKV_B_WIDTH)),
                  _resident((D_MODEL, GATES_WIDTH))],
        out_specs=[row_block(D_MODEL), row_block(QKV_A_WIDTH), row_block(QKV_B_WIDTH),
                   row_block(GATES_WIDTH)],
        out_shape=[jax.ShapeDtypeStruct((rows, D_MODEL), F32),
                   jax.ShapeDtypeStruct((rows, QKV_A_WIDTH), BF16),
                   jax.ShapeDtypeStruct((rows, QKV_B_WIDTH), BF16),
                   jax.ShapeDtypeStruct((rows, GATES_WIDTH), BF16)],
        scratch_shapes=[pltpu.VMEM((ROW_TILE, D_FF), BF16)],
        compiler_params=dense_params,
        name="ffn_in",
    )(x.reshape(rows, D_MODEL), vec(ffn1_norm), ffn1_w_gate.astype(BF16),
      ffn1_w_up.astype(BF16), ffn1_w_down.astype(BF16), vec(mix_norm), w_qkva, w_qkvb, w_gates)

    tab_a = rel_bias.astype(F32)[:, _bias_a_table_index()][:, :, None, :]
    n_pairs = A_WIDTH // LANES
    batch_block = lambda width: pl.BlockSpec((None, seq, width), lambda b: (b, 0, 0))
    ya, yb = pl.pallas_call(
        _attn_kernel,
        grid=(batch,),
        in_specs=[pl.BlockSpec(memory_space=pltpu.SMEM),
                  batch_block(QKV_A_WIDTH), batch_block(QKV_B_WIDTH),
                  _resident((A_HEADS, A_KEY_TILES, 1, 2 * A_QBLOCK)),
                  _resident((B_KV_HEADS, B_GROUP * B_QBLOCK, B_WINDOW))],
        out_specs=[batch_block(A_WIDTH), batch_block(B_Q_WIDTH)],
        out_shape=[jax.ShapeDtypeStruct((batch, seq, A_WIDTH), BF16),
                   jax.ShapeDtypeStruct((batch, seq, B_Q_WIDTH), BF16)],
        scratch_shapes=[pltpu.VMEM((n_pairs, A_KEY_TILES, 2 * A_QBLOCK, A_QBLOCK), F32),
                        pltpu.VMEM((seq, 2 * A_WIDTH), BF16),
                        pltpu.VMEM((seq, 2 * LANES), BF16)],
        compiler_params=pltpu.CompilerParams(dimension_semantics=("arbitrary",),
                                             vmem_limit_bytes=VMEM_LIMIT_BYTES),
        name="attn",
    )(sinks.astype(F32), qkva.reshape(batch, seq, QKV_A_WIDTH),
      qkvb.reshape(batch, seq, QKV_B_WIDTH), tab_a, jnp.asarray(_bias_b_table()))

    gate_block = lambda col: pl.BlockSpec((ROW_TILE, D_MODEL), lambda i, col=col: (i, col))
    out = pl.pallas_call(
        _ffn_out_kernel,
        grid=(n_row_tiles,),
        in_specs=[row_block(D_MODEL), row_block(A_WIDTH), row_block(B_Q_WIDTH),
                  gate_block(0), gate_block(1),
                  _resident((A_WIDTH, D_MODEL)), _resident((B_Q_WIDTH, D_MODEL)),
                  _resident((D_MODEL, D_MODEL)), _resident((1, D_MODEL)),
                  _resident((D_MODEL, D_FF)), _resident((D_MODEL, D_FF)),
                  _resident((D_FF, D_MODEL)), _resident((1, D_MODEL))],
        out_specs=row_block(D_MODEL),
        out_shape=jax.ShapeDtypeStruct((rows, D_MODEL), F32),
        scratch_shapes=[pltpu.VMEM((ROW_TILE, D_FF), BF16)],
        compiler_params=dense_params,
        name="ffn_out",
    )(x1, ya.reshape(rows, A_WIDTH), yb.reshape(rows, B_Q_WIDTH), gates, gates,
      w_proj_a.astype(BF16), w_pb, w_out.astype(BF16), vec(ffn2_norm),
      ffn2_w_gate.astype(BF16), ffn2_w_up.astype(BF16), ffn2_w_down.astype(BF16),
      vec(final_norm))
    return out.reshape(batch, seq, D_MODEL)
```

```python
import functools

import numpy as np
import jax
import jax.numpy as jnp
from jax import lax
from jax.experimental import pallas as pl
from jax.experimental.pallas import tpu as pltpu

F32 = jnp.float32
BF16 = jnp.bfloat16

D_MODEL = 1024
D_FF = 2816
CHUNK = 64
D_HEAD = 64
A_HEADS = 8
A_PREV_CHUNKS = 8
MAX_REL = 128
B_Q_HEADS = 8
B_KV_HEADS = 2
B_GROUP = B_Q_HEADS // B_KV_HEADS
B_PREV_CHUNKS = 2
A_WIDTH = A_HEADS * D_HEAD
B_Q_WIDTH = B_Q_HEADS * D_HEAD
B_KV_WIDTH = B_KV_HEADS * D_HEAD
QKV_A_WIDTH = 3 * A_WIDTH
QKV_B_WIDTH = B_Q_WIDTH + 2 * B_KV_WIDTH
GATES_WIDTH = 2 * D_MODEL
EPS = 1e-6
NEG_INF = -1e30
LOG2_E = float(np.log2(np.e))

LANES = 128
MXU_DIM = 256
VMEM_LIMIT_BYTES = 56 * 1024 * 1024

ROW_TILE = 512
FF_CHUNK = MXU_DIM
A_QBLOCK = 4 * CHUNK
A_KEY_TILES = A_PREV_CHUNKS * CHUNK // A_QBLOCK + 1
B_QBLOCK = 2 * CHUNK
B_WINDOW = 2 * B_QBLOCK


def _rms_norm(x, gain):
    return x * lax.rsqrt(jnp.mean(x * x, axis=-1, keepdims=True) + EPS) * gain


def _swiglu(h, wg_ref, wu_ref, wd_ref, hid_ref):
    for c in range(D_FF // FF_CHUNK):
        cols = slice(c * FF_CHUNK, (c + 1) * FF_CHUNK)
        g = jnp.dot(h, wg_ref[:, cols], preferred_element_type=F32)
        u = jnp.dot(h, wu_ref[:, cols], preferred_element_type=F32)
        hid_ref[:, cols] = (g * jax.nn.sigmoid(g) * u).astype(BF16)
    return jnp.dot(hid_ref[...], wd_ref[...], preferred_element_type=F32)


def _ffn_in_kernel(x_ref, g1_ref, wg_ref, wu_ref, wd_ref, gm_ref, wa_ref, wb_ref, wgt_ref,
                   x1_ref, qkva_ref, qkvb_ref, gates_ref, hid_ref):
    x = x_ref[...]
    h = _rms_norm(x, g1_ref[...]).astype(BF16)
    x1 = x + 0.5 * _swiglu(h, wg_ref, wu_ref, wd_ref, hid_ref)
    x1_ref[...] = x1
    h2 = _rms_norm(x1, gm_ref[...]).astype(BF16)
    qkva_ref[...] = jnp.dot(h2, wa_ref[...], preferred_element_type=F32).astype(BF16)
    qkvb_ref[...] = jnp.dot(h2, wb_ref[...], preferred_element_type=F32).astype(BF16)
    gates_ref[...] = jnp.dot(h2, wgt_ref[...], preferred_element_type=F32).astype(BF16)


def _ffn_out_kernel(x1_ref, ya_ref, yb_ref, ga_ref, gb_ref, wpa_ref, wpb_ref, wo_ref,
                    g2_ref, wg_ref, wu_ref, wd_ref, gf_ref, out_ref, hid_ref):
    pa = jnp.dot(ya_ref[...], wpa_ref[...], preferred_element_type=F32)
    pb = jnp.dot(yb_ref[...], wpb_ref[...], preferred_element_type=F32)
    merged = (jax.nn.sigmoid(ga_ref[...].astype(F32)) * pa
              + jax.nn.sigmoid(gb_ref[...].astype(F32)) * pb)
    x2 = x1_ref[...] + jnp.dot(merged.astype(BF16), wo_ref[...], preferred_element_type=F32)
    h = _rms_norm(x2, g2_ref[...]).astype(BF16)
    x3 = x2 + 0.5 * _swiglu(h, wg_ref, wu_ref, wd_ref, hid_ref)
    out_ref[...] = _rms_norm(x3, gf_ref[...])


def _dot_nt(a, b):
    return lax.dot_general(a, b, (((1,), (1,)), ((), ())), preferred_element_type=F32)


def _attn_kernel(sinks_ref, qkva_ref, qkvb_ref, tab_ref, bias_b_ref, ya_ref, yb_ref,
                 bias_a_ref, vext_a_ref, vext_b_ref):
    seq = qkva_ref.shape[0]
    n_pairs = A_WIDTH // LANES
    low = lax.broadcasted_iota(jnp.int32, (1, LANES), 1) < D_HEAD
    ones = jnp.ones((seq, LANES), BF16)

    @pl.when(pl.program_id(0) == 0)
    def _build_bias_a():
        qc = lax.broadcasted_iota(jnp.int32, (A_QBLOCK, A_QBLOCK), 0) // CHUNK
        kc = lax.broadcasted_iota(jnp.int32, (A_QBLOCK, A_QBLOCK), 1) // CHUNK
        for h in range(A_HEADS):
            for d in range(A_KEY_TILES):
                row = jnp.broadcast_to(tab_ref[h, d], (A_QBLOCK, 2 * A_QBLOCK))
                toeplitz = pltpu.roll(row, 0, 1, stride=1, stride_axis=0)[:, :A_QBLOCK]
                chunk_diff = d * (A_QBLOCK // CHUNK) + qc - kc
                visible = (chunk_diff >= 0) & (chunk_diff <= A_PREV_CHUNKS)
                bias_a_ref[h // 2, d, (h % 2) * A_QBLOCK:(h % 2 + 1) * A_QBLOCK, :] = jnp.where(
                    visible, toeplitz, NEG_INF)

    def mixer_a_block(j, m):
        qrows = slice(m * A_QBLOCK, (m + 1) * A_QBLOCK)
        q = qkva_ref[qrows, j * LANES:(j + 1) * LANES]
        zero = jnp.zeros_like(q)
        qs = jnp.concatenate([jnp.where(low, q, zero), jnp.where(low, zero, q)], axis=0)
        dists = [d for d in range(A_KEY_TILES - 1, -1, -1) if m - d >= 0]
        key_rows = [slice((m - d) * A_QBLOCK, (m - d + 1) * A_QBLOCK) for d in dists]
        kcols = slice(A_WIDTH + j * LANES, A_WIDTH + (j + 1) * LANES)
        s = [_dot_nt(qs, qkva_ref[rows, kcols]) + bias_a_ref[j, d]
             for d, rows in zip(dists, key_rows)]
        mx = jnp.max(functools.reduce(jnp.maximum, s), axis=-1, keepdims=True)
        acc = None
        for t, rows in zip(s, key_rows):
            pv = jnp.dot(jnp.exp2(t - mx).astype(BF16),
                         vext_a_ref[rows, 2 * j * LANES:2 * (j + 1) * LANES],
                         preferred_element_type=F32)
            acc = pv if acc is None else acc + pv
        o = acc[:, :LANES] / acc[:, LANES:]
        ya_ref[qrows, j * LANES:(j + 1) * LANES] = jnp.where(
            low, o[:A_QBLOCK], o[A_QBLOCK:]).astype(BF16)

    def mixer_b_block(m):
        qrows = slice(m * B_QBLOCK, (m + 1) * B_QBLOCK)
        krows = slice(max(m - 1, 0) * B_QBLOCK, (m + 1) * B_QBLOCK)
        bcols = slice(B_WINDOW - (krows.stop - krows.start), B_WINDOW)
        qs = jnp.concatenate([qkvb_ref[qrows, c * LANES:(c + 1) * LANES] for c in range(B_GROUP)],
                             axis=0)
        k = qkvb_ref[krows, B_Q_WIDTH:B_Q_WIDTH + LANES]
        outs = []
        for g in range(B_KV_HEADS):
            kg = jnp.where(low if g == 0 else ~low, k, jnp.zeros_like(k))
            s = _dot_nt(qs, kg) + bias_b_ref[g, :, bcols]
            mx = jnp.maximum(jnp.max(s, axis=-1, keepdims=True), sink[g])
            p = jnp.exp2(s - jnp.concatenate([mx] * (s.shape[1] // LANES), axis=1))
            pv = jnp.dot(p.astype(BF16), vext_b_ref[krows, :], preferred_element_type=F32)
            outs.append(pv[:, :LANES] / (pv[:, LANES:] + jnp.exp2(sink[g] - mx)))
        merged = jnp.where(low, outs[0], outs[1]).astype(BF16)
        for c in range(B_GROUP):
            yb_ref[qrows, c * LANES:(c + 1) * LANES] = merged[c * B_QBLOCK:(c + 1) * B_QBLOCK, :]

    for j in range(n_pairs):
        vcols = slice(2 * A_WIDTH + j * LANES, 2 * A_WIDTH + (j + 1) * LANES)
        vext_a_ref[:, 2 * j * LANES:(2 * j + 1) * LANES] = qkva_ref[:, vcols]
        vext_a_ref[:, (2 * j + 1) * LANES:2 * (j + 1) * LANES] = ones
    vext_b_ref[:, :LANES] = qkvb_ref[:, B_Q_WIDTH + LANES:]
    vext_b_ref[:, LANES:] = ones
    sink = [jnp.concatenate([jnp.full((B_QBLOCK, LANES), sinks_ref[g * B_GROUP + c], F32)
                             for c in range(B_GROUP)], axis=0) for g in range(B_KV_HEADS)]

    b_blocks_per_pair = seq // B_QBLOCK // n_pairs
    for j in range(n_pairs):
        for m in range(seq // A_QBLOCK):
            mixer_a_block(j, m)
        for m in range(j * b_blocks_per_pair, (j + 1) * b_blocks_per_pair):
            mixer_b_block(m)


def _resident(shape):
    return pl.BlockSpec(shape, lambda *_: (0,) * len(shape), pipeline_mode=pl.Buffered(1))


def _bias_a_table_index():
    u = np.arange(2 * A_QBLOCK)
    key_minus_query = np.where(u <= A_QBLOCK, u, u - 2 * A_QBLOCK)
    rel = np.arange(A_KEY_TILES)[:, None] * A_QBLOCK - key_minus_query[None, :]
    return np.clip(rel, -(CHUNK - 1), MAX_REL) + (CHUNK - 1)


def _bias_b_table():
    slopes = np.array([2.0 ** (-8.0 * (h + 1) / B_Q_HEADS) for h in range(B_Q_HEADS)], np.float32)
    i = np.arange(B_QBLOCK)[:, None]
    j = np.arange(B_WINDOW)[None, :]
    rel = i - j + B_QBLOCK
    chunk_diff = i // CHUNK - j // CHUNK + B_QBLOCK // CHUNK
    vis = (chunk_diff >= 0) & (chunk_diff <= B_PREV_CHUNKS)
    dist = np.abs(rel).astype(np.float32)
    out = np.empty((B_KV_HEADS, B_GROUP * B_QBLOCK, B_WINDOW), np.float32)
    for g in range(B_KV_HEADS):
        for c in range(B_GROUP):
            out[g, c * B_QBLOCK:(c + 1) * B_QBLOCK] = np.where(
                vis, -slopes[g * B_GROUP + c] * dist * LOG2_E, NEG_INF)
    return out


def kernel(x, ffn1_norm, ffn1_w_gate, ffn1_w_up, ffn1_w_down, mix_norm, w_in, rel_bias, sinks,
           w_proj_a, w_proj_b, w_out, ffn2_norm, ffn2_w_gate, ffn2_w_up, ffn2_w_down, final_norm):
    batch, seq, d = x.shape
    rows = batch * seq
    assert d == D_MODEL and rows % ROW_TILE == 0 and seq % A_QBLOCK == 0
    n_row_tiles = rows // ROW_TILE
    scale = LOG2_E / np.sqrt(D_HEAD)

    b_perm = np.concatenate([np.arange(h * D_HEAD, (h + 1) * D_HEAD)
                             for c in range(B_GROUP) for h in (c, B_GROUP + c)])
    cuts = np.cumsum([A_WIDTH, A_WIDTH, A_WIDTH, B_Q_WIDTH, B_KV_WIDTH, B_KV_WIDTH, D_MODEL])
    wqa, wka, wva, wqb, wkb, wvb, wga, wgb = jnp.split(w_in, cuts, axis=1)
    w_qkva = jnp.concatenate([wqa * scale, wka, wva], axis=1).astype(BF16)
    w_qkvb = jnp.concatenate([wqb[:, b_perm] * scale, wkb, wvb], axis=1).astype(BF16)
    w_gates = jnp.concatenate([wga, wgb], axis=1).astype(BF16)
    w_pb = w_proj_b[b_perm, :].astype(BF16)
    vec = lambda g: g.reshape(1, D_MODEL).astype(F32)

    row_block = lambda width: pl.BlockSpec((ROW_TILE, width), lambda i: (i, 0))
    dense_params = pltpu.CompilerParams(dimension_semantics=("arbitrary",),
                                        vmem_limit_bytes=VMEM_LIMIT_BYTES)

    x1, qkva, qkvb, gates = pl.pallas_call(
        _ffn_in_kernel,
        grid=(n_row_tiles,),
        in_specs=[row_block(D_MODEL), _resident((1, D_MODEL)),
                  _resident((D_MODEL, D_FF)), _resident((D_MODEL, D_FF)),
                  _resident((D_FF, D_MODEL)), _resident((1, D_MODEL)),
                  _resident((D_MODEL, QKV_A_WIDTH)), _resident((D_MODEL, QKV_B_WIDTH)),
                  _resident((D_MODEL, GATES_WIDTH))],
        out_specs=[row_block(D_MODEL), row_block(QKV_A_WIDTH), row_block(QKV_B_WIDTH),
                   row_block(GATES_WIDTH)],
        out_shape=[jax.ShapeDtypeStruct((rows, D_MODEL), F32),
                   jax.ShapeDtypeStruct((rows, QKV_A_WIDTH), BF16),
                   jax.ShapeDtypeStruct((rows, QKV_B_WIDTH), BF16),
                   jax.ShapeDtypeStruct((rows, GATES_WIDTH), BF16)],
        scratch_shapes=[pltpu.VMEM((ROW_TILE, D_FF), BF16)],
        compiler_params=dense_params,
        name="ffn_in",
    )(x.reshape(rows, D_MODEL), vec(ffn1_norm), ffn1_w_gate.astype(BF16),
      ffn1_w_up.astype(BF16), ffn1_w_down.astype(BF16), vec(mix_norm), w_qkva, w_qkvb, w_gates)

    tab_a = (rel_bias.astype(F32) * LOG2_E)[:, _bias_a_table_index()][:, :, None, :]
    n_pairs = A_WIDTH // LANES
    batch_block = lambda width: pl.BlockSpec((None, seq, width), lambda b: (b, 0, 0))
    ya, yb = pl.pallas_call(
        _attn_kernel,
        grid=(batch,),
        in_specs=[pl.BlockSpec(memory_space=pltpu.SMEM),
                  batch_block(QKV_A_WIDTH), batch_block(QKV_B_WIDTH),
                  _resident((A_HEADS, A_KEY_TILES, 1, 2 * A_QBLOCK)),
                  _resident((B_KV_HEADS, B_GROUP * B_QBLOCK, B_WINDOW))],
        out_specs=[batch_block(A_WIDTH), batch_block(B_Q_WIDTH)],
        out_shape=[jax.ShapeDtypeStruct((batch, seq, A_WIDTH), BF16),
                   jax.ShapeDtypeStruct((batch, seq, B_Q_WIDTH), BF16)],
        scratch_shapes=[pltpu.VMEM((n_pairs, A_KEY_TILES, 2 * A_QBLOCK, A_QBLOCK), F32),
                        pltpu.VMEM((seq, 2 * A_WIDTH), BF16),
                        pltpu.VMEM((seq, 2 * LANES), BF16)],
        compiler_params=pltpu.CompilerParams(dimension_semantics=("arbitrary",),
                                             vmem_limit_bytes=VMEM_LIMIT_BYTES),
        name="attn",
    )(sinks.astype(F32) * LOG2_E, qkva.reshape(batch, seq, QKV_A_WIDTH),
      qkvb.reshape(batch, seq, QKV_B_WIDTH), tab_a, jnp.asarray(_bias_b_table()))

    gate_block = lambda col: pl.BlockSpec((ROW_TILE, D_MODEL), lambda i, col=col: (i, col))
    out = pl.pallas_call(
        _ffn_out_kernel,
        grid=(n_row_tiles,),
        in_specs=[row_block(D_MODEL), row_block(A_WIDTH), row_block(B_Q_WIDTH),
                  gate_block(0), gate_block(1),
                  _resident((A_WIDTH, D_MODEL)), _resident((B_Q_WIDTH, D_MODEL)),
                  _resident((D_MODEL, D_MODEL)), _resident((1, D_MODEL)),
                  _resident((D_MODEL, D_FF)), _resident((D_MODEL, D_FF)),
                  _resident((D_FF, D_MODEL)), _resident((1, D_MODEL))],
        out_specs=row_block(D_MODEL),
        out_shape=jax.ShapeDtypeStruct((rows, D_MODEL), F32),
        scratch_shapes=[pltpu.VMEM((ROW_TILE, D_FF), BF16)],
        compiler_params=dense_params,
        name="ffn_out",
    )(x1, ya.reshape(rows, A_WIDTH), yb.reshape(rows, B_Q_WIDTH), gates, gates,
      w_proj_a.astype(BF16), w_pb, w_out.astype(BF16), vec(ffn2_norm),
      ffn2_w_gate.astype(BF16), ffn2_w_up.astype(BF16), ffn2_w_down.astype(BF16),
      vec(final_norm))
    return out.reshape(batch, seq, D_MODEL)
```

```python
import functools

import numpy as np
import jax
import jax.numpy as jnp
from jax import lax
from jax.experimental import pallas as pl
from jax.experimental.pallas import tpu as pltpu

F32 = jnp.float32
BF16 = jnp.bfloat16

D_MODEL = 1024
D_FF = 2816
CHUNK = 64
D_HEAD = 64
A_HEADS = 8
A_PREV_CHUNKS = 8
MAX_REL = 128
B_Q_HEADS = 8
B_KV_HEADS = 2
B_GROUP = B_Q_HEADS // B_KV_HEADS
B_PREV_CHUNKS = 2
A_WIDTH = A_HEADS * D_HEAD
B_Q_WIDTH = B_Q_HEADS * D_HEAD
B_KV_WIDTH = B_KV_HEADS * D_HEAD
QKV_A_WIDTH = 3 * A_WIDTH
QKV_B_WIDTH = B_Q_WIDTH + 2 * B_KV_WIDTH
GATES_WIDTH = 2 * D_MODEL
EPS = 1e-6
NEG_INF = -1e30
LOG2_E = float(np.log2(np.e))

LANES = 128
MXU_DIM = 256
VMEM_LIMIT_BYTES = 56 * 1024 * 1024

ROW_TILE = 512
FF_CHUNK = MXU_DIM
A_QBLOCK = 4 * CHUNK
A_KEY_TILES = A_PREV_CHUNKS * CHUNK // A_QBLOCK + 1
B_QBLOCK = 2 * CHUNK
B_WINDOW = 2 * B_QBLOCK


def _rms_norm(x, gain):
    return x * lax.rsqrt(jnp.mean(x * x, axis=-1, keepdims=True) + EPS) * gain


def _swiglu(h, wg_ref, wu_ref, wd_ref, hid_ref, between_chunks=None):
    between_chunks = between_chunks or {}
    for c in range(D_FF // FF_CHUNK):
        cols = slice(c * FF_CHUNK, (c + 1) * FF_CHUNK)
        g = jnp.dot(h, wg_ref[:, cols], preferred_element_type=F32)
        u = jnp.dot(h, wu_ref[:, cols], preferred_element_type=F32)
        hid_ref[:, cols] = (g * jax.nn.sigmoid(g) * u).astype(BF16)
        if c in between_chunks:
            between_chunks[c]()
    return jnp.dot(hid_ref[...], wd_ref[...], preferred_element_type=F32)


def _ffn_in_kernel(x_ref, g1_ref, wg_ref, wu_ref, wd_ref, gm_ref, wa_ref, wb_ref, wgt_ref,
                   x1_ref, qkva_ref, qkvb_ref, gates_ref, hid_ref, h2_ref):
    i = pl.program_id(0)
    last = pl.num_programs(0) - 1

    def ffn_stage():
        x = x_ref[...]
        h = _rms_norm(x, g1_ref[...]).astype(BF16)
        x1 = x + 0.5 * _swiglu(h, wg_ref, wu_ref, wd_ref, hid_ref)
        x1_ref[...] = x1
        h2_ref[...] = _rms_norm(x1, gm_ref[...]).astype(BF16)

    def projection_stage():
        h2 = h2_ref[...]
        qkva_ref[...] = jnp.dot(h2, wa_ref[...], preferred_element_type=F32).astype(BF16)
        qkvb_ref[...] = jnp.dot(h2, wb_ref[...], preferred_element_type=F32).astype(BF16)
        gates_ref[...] = jnp.dot(h2, wgt_ref[...], preferred_element_type=F32).astype(BF16)

    @pl.when(i == 0)
    def _first():
        ffn_stage()

    @pl.when((i > 0) & (i < last))
    def _steady():
        projection_stage()
        ffn_stage()

    @pl.when(i == last)
    def _last():
        projection_stage()


def _ffn_out_kernel(x1_ref, ya_ref, yb_ref, ga_ref, gb_ref, wpa_ref, wpb_ref, wo_ref,
                    g2_ref, wg_ref, wu_ref, wd_ref, gf_ref, out_ref, hid_ref, x2_ref, h_ref,
                    mg_ref):
    i = pl.program_id(0)
    last = pl.num_programs(0) - 1
    merged_tile = {}

    def gate_columns(c):
        cols = slice(c * MXU_DIM, (c + 1) * MXU_DIM)
        pa = jnp.dot(ya_ref[...], wpa_ref[:, cols], preferred_element_type=F32)
        pb = jnp.dot(yb_ref[...], wpb_ref[:, cols], preferred_element_type=F32)
        mg_ref[:, cols] = (jax.nn.sigmoid(ga_ref[:, cols].astype(F32)) * pa
                           + jax.nn.sigmoid(gb_ref[:, cols].astype(F32)) * pb).astype(BF16)

    def mix_in():
        x2 = x1_ref[...] + jnp.dot(mg_ref[...], wo_ref[...], preferred_element_type=F32)
        merged_tile["x2"] = x2
        merged_tile["h"] = _rms_norm(x2, g2_ref[...]).astype(BF16)

    def store_merged_tile():
        x2_ref[...] = merged_tile["x2"]
        h_ref[...] = merged_tile["h"]

    merge_pieces = [functools.partial(gate_columns, c) for c in range(D_MODEL // MXU_DIM)]
    merge_pieces.append(mix_in)

    def ffn_stage(between_chunks=None):
        x3 = x2_ref[...] + 0.5 * _swiglu(h_ref[...], wg_ref, wu_ref, wd_ref, hid_ref,
                                         between_chunks)
        out_ref[...] = _rms_norm(x3, gf_ref[...])

    @pl.when(i == 0)
    def _first():
        for piece in merge_pieces:
            piece()
        store_merged_tile()

    @pl.when((i > 0) & (i < last))
    def _steady():
        ffn_stage(dict(zip((0, 1, 2, 3, 5), merge_pieces)))
        store_merged_tile()

    @pl.when(i == last)
    def _last():
        ffn_stage()


def _dot_nt(a, b):
    return lax.dot_general(a, b, (((1,), (1,)), ((), ())), preferred_element_type=F32)


def _attn_kernel(sinks_ref, qkva_ref, qkvb_ref, tab_ref, bias_b_ref, ya_ref, yb_ref,
                 bias_a_ref, vext_a_ref, vext_b_ref):
    seq = qkva_ref.shape[0]
    n_pairs = A_WIDTH // LANES
    low = lax.broadcasted_iota(jnp.int32, (1, LANES), 1) < D_HEAD
    ones = jnp.ones((seq, LANES), BF16)

    @pl.when(pl.program_id(0) == 0)
    def _build_bias_a():
        qc = lax.broadcasted_iota(jnp.int32, (A_QBLOCK, A_QBLOCK), 0) // CHUNK
        kc = lax.broadcasted_iota(jnp.int32, (A_QBLOCK, A_QBLOCK), 1) // CHUNK
        for h in range(A_HEADS):
            for d in range(A_KEY_TILES):
                row = jnp.broadcast_to(tab_ref[h, d], (A_QBLOCK, 2 * A_QBLOCK))
                toeplitz = pltpu.roll(row, 0, 1, stride=1, stride_axis=0)[:, :A_QBLOCK]
                chunk_diff = d * (A_QBLOCK // CHUNK) + qc - kc
                visible = (chunk_diff >= 0) & (chunk_diff <= A_PREV_CHUNKS)
                bias_a_ref[h // 2, d, (h % 2) * A_QBLOCK:(h % 2 + 1) * A_QBLOCK, :] = jnp.where(
                    visible, toeplitz, NEG_INF)

    def mixer_a_block(j, m):
        qrows = slice(m * A_QBLOCK, (m + 1) * A_QBLOCK)
        q = qkva_ref[qrows, j * LANES:(j + 1) * LANES]
        zero = jnp.zeros_like(q)
        qs = jnp.concatenate([jnp.where(low, q, zero), jnp.where(low, zero, q)], axis=0)
        dists = [d for d in range(A_KEY_TILES - 1, -1, -1) if m - d >= 0]
        key_rows = [slice((m - d) * A_QBLOCK, (m - d + 1) * A_QBLOCK) for d in dists]
        kcols = slice(A_WIDTH + j * LANES, A_WIDTH + (j + 1) * LANES)
        s = [_dot_nt(qs, qkva_ref[rows, kcols]) + bias_a_ref[j, d]
             for d, rows in zip(dists, key_rows)]
        mx = jnp.max(functools.reduce(jnp.maximum, s), axis=-1, keepdims=True)
        acc = None
        for t, rows in zip(s, key_rows):
            pv = jnp.dot(jnp.exp2(t - mx).astype(BF16),
                         vext_a_ref[rows, 2 * j * LANES:2 * (j + 1) * LANES],
                         preferred_element_type=F32)
            acc = pv if acc is None else acc + pv
        o = acc[:, :LANES] / acc[:, LANES:]
        ya_ref[qrows, j * LANES:(j + 1) * LANES] = jnp.where(
            low, o[:A_QBLOCK], o[A_QBLOCK:]).astype(BF16)

    def mixer_b_block(m):
        qrows = slice(m * B_QBLOCK, (m + 1) * B_QBLOCK)
        krows = slice(max(m - 1, 0) * B_QBLOCK, (m + 1) * B_QBLOCK)
        bcols = slice(B_WINDOW - (krows.stop - krows.start), B_WINDOW)
        qs = jnp.concatenate([qkvb_ref[qrows, c * LANES:(c + 1) * LANES] for c in range(B_GROUP)],
                             axis=0)
        k = qkvb_ref[krows, B_Q_WIDTH:B_Q_WIDTH + LANES]
        outs = []
        for g in range(B_KV_HEADS):
            kg = jnp.where(low if g == 0 else ~low, k, jnp.zeros_like(k))
            s = _dot_nt(qs, kg) + bias_b_ref[g, :, bcols]
            mx = jnp.maximum(jnp.max(s, axis=-1, keepdims=True), sink[g])
            p = jnp.exp2(s - jnp.concatenate([mx] * (s.shape[1] // LANES), axis=1))
            pv = jnp.dot(p.astype(BF16), vext_b_ref[krows, :], preferred_element_type=F32)
            outs.append(pv[:, :LANES] / (pv[:, LANES:] + jnp.exp2(sink[g] - mx)))
        merged = jnp.where(low, outs[0], outs[1]).astype(BF16)
        for c in range(B_GROUP):
            yb_ref[qrows, c * LANES:(c + 1) * LANES] = merged[c * B_QBLOCK:(c + 1) * B_QBLOCK, :]

    for j in range(n_pairs):
        vcols = slice(2 * A_WIDTH + j * LANES, 2 * A_WIDTH + (j + 1) * LANES)
        vext_a_ref[:, 2 * j * LANES:(2 * j + 1) * LANES] = qkva_ref[:, vcols]
        vext_a_ref[:, (2 * j + 1) * LANES:2 * (j + 1) * LANES] = ones
    vext_b_ref[:, :LANES] = qkvb_ref[:, B_Q_WIDTH + LANES:]
    vext_b_ref[:, LANES:] = ones
    sink = [jnp.concatenate([jnp.full((B_QBLOCK, LANES), sinks_ref[g * B_GROUP + c], F32)
                             for c in range(B_GROUP)], axis=0) for g in range(B_KV_HEADS)]

    b_blocks_per_pair = seq // B_QBLOCK // n_pairs
    for j in range(n_pairs):
        for m in range(seq // A_QBLOCK):
            mixer_a_block(j, m)
        for m in range(j * b_blocks_per_pair, (j + 1) * b_blocks_per_pair):
            mixer_b_block(m)


def _resident(shape):
    return pl.BlockSpec(shape, lambda *_: (0,) * len(shape), pipeline_mode=pl.Buffered(1))


def _bias_a_table_index():
    u = np.arange(2 * A_QBLOCK)
    key_minus_query = np.where(u <= A_QBLOCK, u, u - 2 * A_QBLOCK)
    rel = np.arange(A_KEY_TILES)[:, None] * A_QBLOCK - key_minus_query[None, :]
    return np.clip(rel, -(CHUNK - 1), MAX_REL) + (CHUNK - 1)


def _bias_b_table():
    slopes = np.array([2.0 ** (-8.0 * (h + 1) / B_Q_HEADS) for h in range(B_Q_HEADS)], np.float32)
    i = np.arange(B_QBLOCK)[:, None]
    j = np.arange(B_WINDOW)[None, :]
    rel = i - j + B_QBLOCK
    chunk_diff = i // CHUNK - j // CHUNK + B_QBLOCK // CHUNK
    vis = (chunk_diff >= 0) & (chunk_diff <= B_PREV_CHUNKS)
    dist = np.abs(rel).astype(np.float32)
    out = np.empty((B_KV_HEADS, B_GROUP * B_QBLOCK, B_WINDOW), np.float32)
    for g in range(B_KV_HEADS):
        for c in range(B_GROUP):
            out[g, c * B_QBLOCK:(c + 1) * B_QBLOCK] = np.where(
                vis, -slopes[g * B_GROUP + c] * dist * LOG2_E, NEG_INF)
    return out


def kernel(x, ffn1_norm, ffn1_w_gate, ffn1_w_up, ffn1_w_down, mix_norm, w_in, rel_bias, sinks,
           w_proj_a, w_proj_b, w_out, ffn2_norm, ffn2_w_gate, ffn2_w_up, ffn2_w_down, final_norm):
    batch, seq, d = x.shape
    rows = batch * seq
    assert d == D_MODEL and rows % ROW_TILE == 0 and seq % A_QBLOCK == 0
    n_row_tiles = rows // ROW_TILE
    scale = LOG2_E / np.sqrt(D_HEAD)

    b_perm = np.concatenate([np.arange(h * D_HEAD, (h + 1) * D_HEAD)
                             for c in range(B_GROUP) for h in (c, B_GROUP + c)])
    cuts = np.cumsum([A_WIDTH, A_WIDTH, A_WIDTH, B_Q_WIDTH, B_KV_WIDTH, B_KV_WIDTH, D_MODEL])
    wqa, wka, wva, wqb, wkb, wvb, wga, wgb = jnp.split(w_in, cuts, axis=1)
    w_qkva = jnp.concatenate([wqa * scale, wka, wva], axis=1).astype(BF16)
    w_qkvb = jnp.concatenate([wqb[:, b_perm] * scale, wkb, wvb], axis=1).astype(BF16)
    w_gates = jnp.concatenate([wga, wgb], axis=1).astype(BF16)
    w_pb = w_proj_b[b_perm, :].astype(BF16)
    vec = lambda g: g.reshape(1, D_MODEL).astype(F32)

    dense_params = pltpu.CompilerParams(dimension_semantics=("arbitrary",),
                                        vmem_limit_bytes=VMEM_LIMIT_BYTES)

    stage0_tile = lambda width, col=0: pl.BlockSpec(
        (ROW_TILE, width), lambda i: (jnp.minimum(i, n_row_tiles - 1), col))
    stage1_tile = lambda width: pl.BlockSpec(
        (ROW_TILE, width), lambda i: (jnp.maximum(i - 1, 0), 0))
    x1, qkva, qkvb, gates = pl.pallas_call(
        _ffn_in_kernel,
        grid=(n_row_tiles + 1,),
        in_specs=[stage0_tile(D_MODEL), _resident((1, D_MODEL)),
                  _resident((D_MODEL, D_FF)), _resident((D_MODEL, D_FF)),
                  _resident((D_FF, D_MODEL)), _resident((1, D_MODEL)),
                  _resident((D_MODEL, QKV_A_WIDTH)), _resident((D_MODEL, QKV_B_WIDTH)),
                  _resident((D_MODEL, GATES_WIDTH))],
        out_specs=[stage0_tile(D_MODEL), stage1_tile(QKV_A_WIDTH), stage1_tile(QKV_B_WIDTH),
                   stage1_tile(GATES_WIDTH)],
        out_shape=[jax.ShapeDtypeStruct((rows, D_MODEL), F32),
                   jax.ShapeDtypeStruct((rows, QKV_A_WIDTH), BF16),
                   jax.ShapeDtypeStruct((rows, QKV_B_WIDTH), BF16),
                   jax.ShapeDtypeStruct((rows, GATES_WIDTH), BF16)],
        scratch_shapes=[pltpu.VMEM((ROW_TILE, D_FF), BF16), pltpu.VMEM((ROW_TILE, D_MODEL), BF16)],
        compiler_params=dense_params,
        name="ffn_in",
    )(x.reshape(rows, D_MODEL), vec(ffn1_norm), ffn1_w_gate.astype(BF16),
      ffn1_w_up.astype(BF16), ffn1_w_down.astype(BF16), vec(mix_norm), w_qkva, w_qkvb, w_gates)

    tab_a = (rel_bias.astype(F32) * LOG2_E)[:, _bias_a_table_index()][:, :, None, :]
    n_pairs = A_WIDTH // LANES
    batch_block = lambda width: pl.BlockSpec((None, seq, width), lambda b: (b, 0, 0))
    ya, yb = pl.pallas_call(
        _attn_kernel,
        grid=(batch,),
        in_specs=[pl.BlockSpec(memory_space=pltpu.SMEM),
                  batch_block(QKV_A_WIDTH), batch_block(QKV_B_WIDTH),
                  _resident((A_HEADS, A_KEY_TILES, 1, 2 * A_QBLOCK)),
                  _resident((B_KV_HEADS, B_GROUP * B_QBLOCK, B_WINDOW))],
        out_specs=[batch_block(A_WIDTH), batch_block(B_Q_WIDTH)],
        out_shape=[jax.ShapeDtypeStruct((batch, seq, A_WIDTH), BF16),
                   jax.ShapeDtypeStruct((batch, seq, B_Q_WIDTH), BF16)],
        scratch_shapes=[pltpu.VMEM((n_pairs, A_KEY_TILES, 2 * A_QBLOCK, A_QBLOCK), F32),
                        pltpu.VMEM((seq, 2 * A_WIDTH), BF16),
                        pltpu.VMEM((seq, 2 * LANES), BF16)],
        compiler_params=pltpu.CompilerParams(dimension_semantics=("arbitrary",),
                                             vmem_limit_bytes=VMEM_LIMIT_BYTES),
        name="attn",
    )(sinks.astype(F32) * LOG2_E, qkva.reshape(batch, seq, QKV_A_WIDTH),
      qkvb.reshape(batch, seq, QKV_B_WIDTH), tab_a, jnp.asarray(_bias_b_table()))

    out = pl.pallas_call(
        _ffn_out_kernel,
        grid=(n_row_tiles + 1,),
        in_specs=[stage0_tile(D_MODEL), stage0_tile(A_WIDTH), stage0_tile(B_Q_WIDTH),
                  stage0_tile(D_MODEL, col=0), stage0_tile(D_MODEL, col=1),
                  _resident((A_WIDTH, D_MODEL)), _resident((B_Q_WIDTH, D_MODEL)),
                  _resident((D_MODEL, D_MODEL)), _resident((1, D_MODEL)),
                  _resident((D_MODEL, D_FF)), _resident((D_MODEL, D_FF)),
                  _resident((D_FF, D_MODEL)), _resident((1, D_MODEL))],
        out_specs=stage1_tile(D_MODEL),
        out_shape=jax.ShapeDtypeStruct((rows, D_MODEL), F32),
        scratch_shapes=[pltpu.VMEM((ROW_TILE, D_FF), BF16), pltpu.VMEM((ROW_TILE, D_MODEL), F32),
                        pltpu.VMEM((ROW_TILE, D_MODEL), BF16), pltpu.VMEM((ROW_TILE, D_MODEL), BF16)],
        compiler_params=dense_params,
        name="ffn_out",
    )(x1, ya.reshape(rows, A_WIDTH), yb.reshape(rows, B_Q_WIDTH), gates, gates,
      w_proj_a.astype(BF16), w_pb, w_out.astype(BF16), vec(ffn2_norm),
      ffn2_w_gate.astype(BF16), ffn2_w_up.astype(BF16), ffn2_w_down.astype(BF16),
      vec(final_norm))
    return out.reshape(batch, seq, D_MODEL)
```

```python
import functools

import numpy as np
import jax
import jax.numpy as jnp
from jax import lax
from jax.experimental import pallas as pl
from jax.experimental.pallas import tpu as pltpu

F32 = jnp.float32
BF16 = jnp.bfloat16

D_MODEL = 1024
D_FF = 2816
CHUNK = 64
D_HEAD = 64
A_HEADS = 8
A_PREV_CHUNKS = 8
MAX_REL = 128
B_Q_HEADS = 8
B_KV_HEADS = 2
B_GROUP = B_Q_HEADS // B_KV_HEADS
B_PREV_CHUNKS = 2
A_WIDTH = A_HEADS * D_HEAD
B_Q_WIDTH = B_Q_HEADS * D_HEAD
B_KV_WIDTH = B_KV_HEADS * D_HEAD
QKV_A_WIDTH = 3 * A_WIDTH
QKV_B_WIDTH = B_Q_WIDTH + 2 * B_KV_WIDTH
GATES_WIDTH = 2 * D_MODEL
EPS = 1e-6
NEG_INF = -1e30
LOG2_E = float(np.log2(np.e))

LANES = 128
MXU_DIM = 256
VMEM_LIMIT_BYTES = 56 * 1024 * 1024

ROW_TILE = 512
FF_CHUNK = MXU_DIM
A_QBLOCK = 4 * CHUNK
A_KEY_TILES = A_PREV_CHUNKS * CHUNK // A_QBLOCK + 1
B_QBLOCK = 2 * CHUNK
B_WINDOW = 2 * B_QBLOCK


def _rms_norm(x, gain):
    return x * lax.rsqrt(jnp.mean(x * x, axis=-1, keepdims=True) + EPS) * gain


def _swiglu(h, wg_ref, wu_ref, wd_ref, hid_ref, between_chunks=None):
    between_chunks = between_chunks or {}
    for c in range(D_FF // FF_CHUNK):
        cols = slice(c * FF_CHUNK, (c + 1) * FF_CHUNK)
        g = jnp.dot(h, wg_ref[:, cols], preferred_element_type=F32)
        u = jnp.dot(h, wu_ref[:, cols], preferred_element_type=F32)
        hid_ref[:, cols] = (g * jax.nn.sigmoid(g) * u).astype(BF16)
        if c in between_chunks:
            between_chunks[c]()
    return jnp.dot(hid_ref[...], wd_ref[...], preferred_element_type=F32)


def _ffn_in_kernel(x_ref, g1_ref, wg_ref, wu_ref, wd_ref, gm_ref, wa_ref, wb_ref, wgt_ref,
                   x1_ref, qkva_ref, qkvb_ref, gates_ref, hid_ref, h2_ref):
    i = pl.program_id(0)
    last = pl.num_programs(0) - 1

    def ffn_stage():
        x = x_ref[...]
        h = _rms_norm(x, g1_ref[...]).astype(BF16)
        x1 = x + 0.5 * _swiglu(h, wg_ref, wu_ref, wd_ref, hid_ref)
        x1_ref[...] = x1
        h2_ref[...] = _rms_norm(x1, gm_ref[...]).astype(BF16)

    def projection_stage():
        h2 = h2_ref[...]
        qkva_ref[...] = jnp.dot(h2, wa_ref[...], preferred_element_type=F32).astype(BF16)
        qkvb_ref[...] = jnp.dot(h2, wb_ref[...], preferred_element_type=F32).astype(BF16)
        gates_ref[...] = jnp.dot(h2, wgt_ref[...], preferred_element_type=F32).astype(BF16)

    @pl.when(i == 0)
    def _first():
        ffn_stage()

    @pl.when((i > 0) & (i < last))
    def _steady():
        projection_stage()
        ffn_stage()

    @pl.when(i == last)
    def _last():
        projection_stage()


def _ffn_out_kernel(x1_ref, ya_ref, yb_ref, ga_ref, gb_ref, wpa_ref, wpb_ref, wo_ref,
                    g2_ref, wg_ref, wu_ref, wd_ref, gf_ref, out_ref, hid_ref, x2_ref, h_ref,
                    mg_ref):
    i = pl.program_id(0)
    last = pl.num_programs(0) - 1
    merged_tile = {}

    def gate_columns(c):
        cols = slice(c * MXU_DIM, (c + 1) * MXU_DIM)
        pa = jnp.dot(ya_ref[...], wpa_ref[:, cols], preferred_element_type=F32)
        pb = jnp.dot(yb_ref[...], wpb_ref[:, cols], preferred_element_type=F32)
        mg_ref[:, cols] = (jax.nn.sigmoid(ga_ref[:, cols].astype(F32)) * pa
                           + jax.nn.sigmoid(gb_ref[:, cols].astype(F32)) * pb).astype(BF16)

    def mix_in():
        x2 = x1_ref[...] + jnp.dot(mg_ref[...], wo_ref[...], preferred_element_type=F32)
        merged_tile["x2"] = x2
        merged_tile["h"] = _rms_norm(x2, g2_ref[...]).astype(BF16)

    def store_merged_tile():
        x2_ref[...] = merged_tile["x2"]
        h_ref[...] = merged_tile["h"]

    merge_pieces = [functools.partial(gate_columns, c) for c in range(D_MODEL // MXU_DIM)]
    merge_pieces.append(mix_in)

    def ffn_stage(between_chunks=None):
        x3 = x2_ref[...] + 0.5 * _swiglu(h_ref[...], wg_ref, wu_ref, wd_ref, hid_ref,
                                         between_chunks)
        out_ref[...] = _rms_norm(x3, gf_ref[...])

    @pl.when(i == 0)
    def _first():
        for piece in merge_pieces:
            piece()
        store_merged_tile()

    @pl.when((i > 0) & (i < last))
    def _steady():
        ffn_stage(dict(zip((0, 1, 2, 3, 5), merge_pieces)))
        store_merged_tile()

    @pl.when(i == last)
    def _last():
        ffn_stage()


def _dot_nt(a, b):
    return lax.dot_general(a, b, (((1,), (1,)), ((), ())), preferred_element_type=F32)


def _attn_kernel(sinks_ref, qkva_ref, qkvb_ref, tab_ref, bias_b_ref, ya_ref, yb_ref,
                 bias_a_ref, vext_a_ref, vext_b_ref):
    seq = qkva_ref.shape[0]
    n_pairs = A_WIDTH // LANES
    low = lax.broadcasted_iota(jnp.int32, (1, LANES), 1) < D_HEAD
    ones = jnp.ones((seq, LANES), BF16)

    @pl.when(pl.program_id(0) == 0)
    def _build_bias_a():
        qc = lax.broadcasted_iota(jnp.int32, (A_QBLOCK, A_QBLOCK), 0) // CHUNK
        kc = lax.broadcasted_iota(jnp.int32, (A_QBLOCK, A_QBLOCK), 1) // CHUNK
        for h in range(A_HEADS):
            for d in range(A_KEY_TILES):
                row = jnp.broadcast_to(tab_ref[h, d], (A_QBLOCK, 2 * A_QBLOCK))
                toeplitz = pltpu.roll(row, 0, 1, stride=1, stride_axis=0)[:, :A_QBLOCK]
                chunk_diff = d * (A_QBLOCK // CHUNK) + qc - kc
                visible = (chunk_diff >= 0) & (chunk_diff <= A_PREV_CHUNKS)
                bias_a_ref[h // 2, d, (h % 2) * A_QBLOCK:(h % 2 + 1) * A_QBLOCK, :] = jnp.where(
                    visible, toeplitz, NEG_INF)

    def mixer_a_scores(j, m):
        q = qkva_ref[m * A_QBLOCK:(m + 1) * A_QBLOCK, j * LANES:(j + 1) * LANES]
        zero = jnp.zeros_like(q)
        qs = jnp.concatenate([jnp.where(low, q, zero), jnp.where(low, zero, q)], axis=0)
        kcols = slice(A_WIDTH + j * LANES, A_WIDTH + (j + 1) * LANES)
        dists = [d for d in range(A_KEY_TILES - 1, -1, -1) if m - d >= 0]
        key_rows = [slice((m - d) * A_QBLOCK, (m - d + 1) * A_QBLOCK) for d in dists]
        return [(_dot_nt(qs, qkva_ref[rows, kcols]) + bias_a_ref[j, d], rows)
                for d, rows in zip(dists, key_rows)]

    def mixer_a_finish(j, m, scores):
        mx = jnp.max(functools.reduce(jnp.maximum, [t for t, _ in scores]), axis=-1, keepdims=True)
        acc = None
        for t, rows in scores:
            pv = jnp.dot(jnp.exp2(t - mx).astype(BF16),
                         vext_a_ref[rows, 2 * j * LANES:2 * (j + 1) * LANES],
                         preferred_element_type=F32)
            acc = pv if acc is None else acc + pv
        o = acc[:, :LANES] / acc[:, LANES:]
        ya_ref[m * A_QBLOCK:(m + 1) * A_QBLOCK, j * LANES:(j + 1) * LANES] = jnp.where(
            low, o[:A_QBLOCK], o[A_QBLOCK:]).astype(BF16)

    def mixer_b_scores(m):
        krows = slice(max(m - 1, 0) * B_QBLOCK, (m + 1) * B_QBLOCK)
        bcols = slice(B_WINDOW - (krows.stop - krows.start), B_WINDOW)
        qs = jnp.concatenate([qkvb_ref[m * B_QBLOCK:(m + 1) * B_QBLOCK, c * LANES:(c + 1) * LANES]
                              for c in range(B_GROUP)], axis=0)
        k = qkvb_ref[krows, B_Q_WIDTH:B_Q_WIDTH + LANES]
        zero = jnp.zeros_like(k)
        return [_dot_nt(qs, jnp.where(low, k, zero) if g == 0 else jnp.where(low, zero, k))
                + bias_b_ref[g, :, bcols] for g in range(B_KV_HEADS)]

    def mixer_b_finish(m, scores):
        qrows = slice(m * B_QBLOCK, (m + 1) * B_QBLOCK)
        krows = slice(max(m - 1, 0) * B_QBLOCK, (m + 1) * B_QBLOCK)
        outs = []
        for g, s in enumerate(scores):
            mx = jnp.maximum(jnp.max(s, axis=-1, keepdims=True), sink[g])
            p = jnp.exp2(s - jnp.concatenate([mx] * (s.shape[1] // LANES), axis=1))
            pv = jnp.dot(p.astype(BF16), vext_b_ref[krows, :], preferred_element_type=F32)
            outs.append(pv[:, :LANES] / (pv[:, LANES:] + jnp.exp2(sink[g] - mx)))
        merged = jnp.where(low, outs[0], outs[1]).astype(BF16)
        for c in range(B_GROUP):
            yb_ref[qrows, c * LANES:(c + 1) * LANES] = merged[c * B_QBLOCK:(c + 1) * B_QBLOCK, :]

    for j in range(n_pairs):
        vcols = slice(2 * A_WIDTH + j * LANES, 2 * A_WIDTH + (j + 1) * LANES)
        vext_a_ref[:, 2 * j * LANES:(2 * j + 1) * LANES] = qkva_ref[:, vcols]
        vext_a_ref[:, (2 * j + 1) * LANES:2 * (j + 1) * LANES] = ones
    vext_b_ref[:, :LANES] = qkvb_ref[:, B_Q_WIDTH + LANES:]
    vext_b_ref[:, LANES:] = ones
    sink = [jnp.concatenate([jnp.full((B_QBLOCK, LANES), sinks_ref[g * B_GROUP + c], F32)
                             for c in range(B_GROUP)], axis=0) for g in range(B_KV_HEADS)]

    b_blocks_per_pair = seq // B_QBLOCK // n_pairs
    blocks = []
    for j in range(n_pairs):
        blocks += [(functools.partial(mixer_a_scores, j, m), functools.partial(mixer_a_finish, j, m))
                   for m in range(seq // A_QBLOCK)]
        blocks += [(functools.partial(mixer_b_scores, m), functools.partial(mixer_b_finish, m))
                   for m in range(j * b_blocks_per_pair, (j + 1) * b_blocks_per_pair)]
    scores = blocks[0][0]()
    for n, (_, finish) in enumerate(blocks):
        next_scores = blocks[n + 1][0]() if n + 1 < len(blocks) else None
        finish(scores)
        scores = next_scores


def _resident(shape):
    return pl.BlockSpec(shape, lambda *_: (0,) * len(shape), pipeline_mode=pl.Buffered(1))


def _bias_a_table_index():
    u = np.arange(2 * A_QBLOCK)
    key_minus_query = np.where(u <= A_QBLOCK, u, u - 2 * A_QBLOCK)
    rel = np.arange(A_KEY_TILES)[:, None] * A_QBLOCK - key_minus_query[None, :]
    return np.clip(rel, -(CHUNK - 1), MAX_REL) + (CHUNK - 1)


def _bias_b_table():
    slopes = np.array([2.0 ** (-8.0 * (h + 1) / B_Q_HEADS) for h in range(B_Q_HEADS)], np.float32)
    i = np.arange(B_QBLOCK)[:, None]
    j = np.arange(B_WINDOW)[None, :]
    rel = i - j + B_QBLOCK
    chunk_diff = i // CHUNK - j // CHUNK + B_QBLOCK // CHUNK
    vis = (chunk_diff >= 0) & (chunk_diff <= B_PREV_CHUNKS)
    dist = np.abs(rel).astype(np.float32)
    out = np.empty((B_KV_HEADS, B_GROUP * B_QBLOCK, B_WINDOW), np.float32)
    for g in range(B_KV_HEADS):
        for c in range(B_GROUP):
            out[g, c * B_QBLOCK:(c + 1) * B_QBLOCK] = np.where(
                vis, -slopes[g * B_GROUP + c] * dist * LOG2_E, NEG_INF)
    return out


def kernel(x, ffn1_norm, ffn1_w_gate, ffn1_w_up, ffn1_w_down, mix_norm, w_in, rel_bias, sinks,
           w_proj_a, w_proj_b, w_out, ffn2_norm, ffn2_w_gate, ffn2_w_up, ffn2_w_down, final_norm):
    batch, seq, d = x.shape
    rows = batch * seq
    assert d == D_MODEL and rows % ROW_TILE == 0 and seq % A_QBLOCK == 0
    n_row_tiles = rows // ROW_TILE
    scale = LOG2_E / np.sqrt(D_HEAD)

    b_perm = np.concatenate([np.arange(h * D_HEAD, (h + 1) * D_HEAD)
                             for c in range(B_GROUP) for h in (c, B_GROUP + c)])
    cuts = np.cumsum([A_WIDTH, A_WIDTH, A_WIDTH, B_Q_WIDTH, B_KV_WIDTH, B_KV_WIDTH, D_MODEL])
    wqa, wka, wva, wqb, wkb, wvb, wga, wgb = jnp.split(w_in, cuts, axis=1)
    w_qkva = jnp.concatenate([wqa * scale, wka, wva], axis=1).astype(BF16)
    w_qkvb = jnp.concatenate([wqb[:, b_perm] * scale, wkb, wvb], axis=1).astype(BF16)
    w_gates = jnp.concatenate([wga, wgb], axis=1).astype(BF16)
    w_pb = w_proj_b[b_perm, :].astype(BF16)
    vec = lambda g: g.reshape(1, D_MODEL).astype(F32)

    dense_params = pltpu.CompilerParams(dimension_semantics=("arbitrary",),
                                        vmem_limit_bytes=VMEM_LIMIT_BYTES)

    stage0_tile = lambda width, col=0: pl.BlockSpec(
        (ROW_TILE, width), lambda i: (jnp.minimum(i, n_row_tiles - 1), col))
    stage1_tile = lambda width: pl.BlockSpec(
        (ROW_TILE, width), lambda i: (jnp.maximum(i - 1, 0), 0))
    x1, qkva, qkvb, gates = pl.pallas_call(
        _ffn_in_kernel,
        grid=(n_row_tiles + 1,),
        in_specs=[stage0_tile(D_MODEL), _resident((1, D_MODEL)),
                  _resident((D_MODEL, D_FF)), _resident((D_MODEL, D_FF)),
                  _resident((D_FF, D_MODEL)), _resident((1, D_MODEL)),
                  _resident((D_MODEL, QKV_A_WIDTH)), _resident((D_MODEL, QKV_B_WIDTH)),
                  _resident((D_MODEL, GATES_WIDTH))],
        out_specs=[stage0_tile(D_MODEL), stage1_tile(QKV_A_WIDTH), stage1_tile(QKV_B_WIDTH),
                   stage1_tile(GATES_WIDTH)],
        out_shape=[jax.ShapeDtypeStruct((rows, D_MODEL), F32),
                   jax.ShapeDtypeStruct((rows, QKV_A_WIDTH), BF16),
                   jax.ShapeDtypeStruct((rows, QKV_B_WIDTH), BF16),
                   jax.ShapeDtypeStruct((rows, GATES_WIDTH), BF16)],
        scratch_shapes=[pltpu.VMEM((ROW_TILE, D_FF), BF16), pltpu.VMEM((ROW_TILE, D_MODEL), BF16)],
        compiler_params=dense_params,
        name="ffn_in",
    )(x.reshape(rows, D_MODEL), vec(ffn1_norm), ffn1_w_gate.astype(BF16),
      ffn1_w_up.astype(BF16), ffn1_w_down.astype(BF16), vec(mix_norm), w_qkva, w_qkvb, w_gates)

    tab_a = (rel_bias.astype(F32) * LOG2_E)[:, _bias_a_table_index()][:, :, None, :]
    n_pairs = A_WIDTH // LANES
    batch_block = lambda width: pl.BlockSpec((None, seq, width), lambda b: (b, 0, 0))
    ya, yb = pl.pallas_call(
        _attn_kernel,
        grid=(batch,),
        in_specs=[pl.BlockSpec(memory_space=pltpu.SMEM),
                  batch_block(QKV_A_WIDTH), batch_block(QKV_B_WIDTH),
                  _resident((A_HEADS, A_KEY_TILES, 1, 2 * A_QBLOCK)),
                  _resident((B_KV_HEADS, B_GROUP * B_QBLOCK, B_WINDOW))],
        out_specs=[batch_block(A_WIDTH), batch_block(B_Q_WIDTH)],
        out_shape=[jax.ShapeDtypeStruct((batch, seq, A_WIDTH), BF16),
                   jax.ShapeDtypeStruct((batch, seq, B_Q_WIDTH), BF16)],
        scratch_shapes=[pltpu.VMEM((n_pairs, A_KEY_TILES, 2 * A_QBLOCK, A_QBLOCK), F32),
                        pltpu.VMEM((seq, 2 * A_WIDTH), BF16),
                        pltpu.VMEM((seq, 2 * LANES), BF16)],
        compiler_params=pltpu.CompilerParams(dimension_semantics=("arbitrary",),
                                             vmem_limit_bytes=VMEM_LIMIT_BYTES),
        name="attn",
    )(sinks.astype(F32) * LOG2_E, qkva.reshape(batch, seq, QKV_A_WIDTH),
      qkvb.reshape(batch, seq, QKV_B_WIDTH), tab_a, jnp.asarray(_bias_b_table()))

    out = pl.pallas_call(
        _ffn_out_kernel,
        grid=(n_row_tiles + 1,),
        in_specs=[stage0_tile(D_MODEL), stage0_tile(A_WIDTH), stage0_tile(B_Q_WIDTH),
                  stage0_tile(D_MODEL, col=0), stage0_tile(D_MODEL, col=1),
                  _resident((A_WIDTH, D_MODEL)), _resident((B_Q_WIDTH, D_MODEL)),
                  _resident((D_MODEL, D_MODEL)), _resident((1, D_MODEL)),
                  _resident((D_MODEL, D_FF)), _resident((D_MODEL, D_FF)),
                  _resident((D_FF, D_MODEL)), _resident((1, D_MODEL))],
        out_specs=stage1_tile(D_MODEL),
        out_shape=jax.ShapeDtypeStruct((rows, D_MODEL), F32),
        scratch_shapes=[pltpu.VMEM((ROW_TILE, D_FF), BF16), pltpu.VMEM((ROW_TILE, D_MODEL), F32),
                        pltpu.VMEM((ROW_TILE, D_MODEL), BF16), pltpu.VMEM((ROW_TILE, D_MODEL), BF16)],
        compiler_params=dense_params,
        name="ffn_out",
    )(x1, ya.reshape(rows, A_WIDTH), yb.reshape(rows, B_Q_WIDTH), gates, gates,
      w_proj_a.astype(BF16), w_pb, w_out.astype(BF16), vec(ffn2_norm),
      ffn2_w_gate.astype(BF16), ffn2_w_up.astype(BF16), ffn2_w_down.astype(BF16),
      vec(final_norm))
    return out.reshape(batch, seq, D_MODEL)
```

```python
import functools

import numpy as np
import jax
import jax.numpy as jnp
from jax import lax
from jax.experimental import pallas as pl
from jax.experimental.pallas import tpu as pltpu

F32 = jnp.float32
BF16 = jnp.bfloat16

D_MODEL = 1024
D_FF = 2816
CHUNK = 64
D_HEAD = 64
A_HEADS = 8
A_PREV_CHUNKS = 8
MAX_REL = 128
B_Q_HEADS = 8
B_KV_HEADS = 2
B_GROUP = B_Q_HEADS // B_KV_HEADS
B_PREV_CHUNKS = 2
A_WIDTH = A_HEADS * D_HEAD
B_Q_WIDTH = B_Q_HEADS * D_HEAD
B_KV_WIDTH = B_KV_HEADS * D_HEAD
QKV_A_WIDTH = 3 * A_WIDTH
QKV_B_WIDTH = B_Q_WIDTH + 2 * B_KV_WIDTH
GATES_WIDTH = 2 * D_MODEL
EPS = 1e-6
NEG_INF = -1e30
LOG2_E = float(np.log2(np.e))

LANES = 128
MXU_DIM = 256
VMEM_LIMIT_BYTES = 56 * 1024 * 1024

ROW_TILE = 512
FF_CHUNK = MXU_DIM
A_QBLOCK = 4 * CHUNK
A_KEY_TILES = A_PREV_CHUNKS * CHUNK // A_QBLOCK + 1
B_QBLOCK = 2 * CHUNK
B_WINDOW = 2 * B_QBLOCK
assert B_GROUP * B_QBLOCK == 2 * A_QBLOCK and B_WINDOW == A_QBLOCK and B_KV_HEADS <= A_KEY_TILES


def _rms_norm(x, gain):
    return x * lax.rsqrt(jnp.mean(x * x, axis=-1, keepdims=True) + EPS) * gain


def _swiglu(h, wg_ref, wu_ref, wd_ref, hid_ref, between_chunks=None):
    between_chunks = between_chunks or {}
    for c in range(D_FF // FF_CHUNK):
        cols = slice(c * FF_CHUNK, (c + 1) * FF_CHUNK)
        g = jnp.dot(h, wg_ref[:, cols], preferred_element_type=F32)
        u = jnp.dot(h, wu_ref[:, cols], preferred_element_type=F32)
        hid_ref[:, cols] = (g * jax.nn.sigmoid(g) * u).astype(BF16)
        if c in between_chunks:
            between_chunks[c]()
    return jnp.dot(hid_ref[...], wd_ref[...], preferred_element_type=F32)


def _ffn_in_kernel(x_ref, g1_ref, wg_ref, wu_ref, wd_ref, gm_ref, wa_ref, wb_ref, wgt_ref,
                   x1_ref, qkva_ref, qkvb_ref, gates_ref, hid_ref, h2_ref):
    i = pl.program_id(0)
    last = pl.num_programs(0) - 1

    def ffn_stage():
        x = x_ref[...]
        h = _rms_norm(x, g1_ref[...]).astype(BF16)
        x1 = x + 0.5 * _swiglu(h, wg_ref, wu_ref, wd_ref, hid_ref)
        x1_ref[...] = x1
        h2_ref[...] = _rms_norm(x1, gm_ref[...]).astype(BF16)

    def projection_stage():
        h2 = h2_ref[...]
        qkva_ref[...] = jnp.dot(h2, wa_ref[...], preferred_element_type=F32).astype(BF16)
        qkvb_ref[...] = jnp.dot(h2, wb_ref[...], preferred_element_type=F32).astype(BF16)
        gates_ref[...] = jnp.dot(h2, wgt_ref[...], preferred_element_type=F32).astype(BF16)

    @pl.when(i == 0)
    def _first():
        ffn_stage()

    @pl.when((i > 0) & (i < last))
    def _steady():
        projection_stage()
        ffn_stage()

    @pl.when(i == last)
    def _last():
        projection_stage()


def _ffn_out_kernel(x1_ref, ya_ref, yb_ref, ga_ref, gb_ref, wpa_ref, wpb_ref, wo_ref,
                    g2_ref, wg_ref, wu_ref, wd_ref, gf_ref, out_ref, hid_ref, x2_ref, h_ref,
                    mg_ref):
    i = pl.program_id(0)
    last = pl.num_programs(0) - 1
    merged_tile = {}

    def gate_columns(c):
        cols = slice(c * MXU_DIM, (c + 1) * MXU_DIM)
        pa = jnp.dot(ya_ref[...], wpa_ref[:, cols], preferred_element_type=F32)
        pb = jnp.dot(yb_ref[...], wpb_ref[:, cols], preferred_element_type=F32)
        mg_ref[:, cols] = (jax.nn.sigmoid(ga_ref[:, cols].astype(F32)) * pa
                           + jax.nn.sigmoid(gb_ref[:, cols].astype(F32)) * pb).astype(BF16)

    def mix_in():
        x2 = x1_ref[...] + jnp.dot(mg_ref[...], wo_ref[...], preferred_element_type=F32)
        merged_tile["x2"] = x2
        merged_tile["h"] = _rms_norm(x2, g2_ref[...]).astype(BF16)

    def store_merged_tile():
        x2_ref[...] = merged_tile["x2"]
        h_ref[...] = merged_tile["h"]

    merge_pieces = [functools.partial(gate_columns, c) for c in range(D_MODEL // MXU_DIM)]
    merge_pieces.append(mix_in)

    def ffn_stage(between_chunks=None):
        x3 = x2_ref[...] + 0.5 * _swiglu(h_ref[...], wg_ref, wu_ref, wd_ref, hid_ref,
                                         between_chunks)
        out_ref[...] = _rms_norm(x3, gf_ref[...])

    @pl.when(i == 0)
    def _first():
        for piece in merge_pieces:
            piece()
        store_merged_tile()

    @pl.when((i > 0) & (i < last))
    def _steady():
        ffn_stage(dict(zip((0, 1, 2, 3, 5), merge_pieces)))
        store_merged_tile()

    @pl.when(i == last)
    def _last():
        ffn_stage()


def _dot_nt(a, b):
    return lax.dot_general(a, b, (((1,), (1,)), ((), ())), preferred_element_type=F32)


def _attn_kernel(sinks_ref, qkva_ref, qkvb_ref, tab_ref, bias_b_ref, ya_ref, yb_ref,
                 bias_a_ref, vext_a_ref, vext_b_ref, s_ref):
    seq = qkva_ref.shape[0]
    n_pairs = A_WIDTH // LANES
    low = lax.broadcasted_iota(jnp.int32, (1, LANES), 1) < D_HEAD
    ones = jnp.ones((seq, LANES), BF16)

    @pl.when(pl.program_id(0) == 0)
    def _build_bias_a():
        qc = lax.broadcasted_iota(jnp.int32, (A_QBLOCK, A_QBLOCK), 0) // CHUNK
        kc = lax.broadcasted_iota(jnp.int32, (A_QBLOCK, A_QBLOCK), 1) // CHUNK
        for h in range(A_HEADS):
            for d in range(A_KEY_TILES):
                row = jnp.broadcast_to(tab_ref[h, d], (A_QBLOCK, 2 * A_QBLOCK))
                toeplitz = pltpu.roll(row, 0, 1, stride=1, stride_axis=0)[:, :A_QBLOCK]
                chunk_diff = d * (A_QBLOCK // CHUNK) + qc - kc
                visible = (chunk_diff >= 0) & (chunk_diff <= A_PREV_CHUNKS)
                bias_a_ref[h // 2, d, (h % 2) * A_QBLOCK:(h % 2 + 1) * A_QBLOCK, :] = jnp.where(
                    visible, toeplitz, NEG_INF)

    def mixer_a_scores(slot, j, m):
        q = qkva_ref[m * A_QBLOCK:(m + 1) * A_QBLOCK, j * LANES:(j + 1) * LANES]
        zero = jnp.zeros_like(q)
        qs = jnp.concatenate([jnp.where(low, q, zero), jnp.where(low, zero, q)], axis=0)
        kcols = slice(A_WIDTH + j * LANES, A_WIDTH + (j + 1) * LANES)
        dists = [d for d in range(A_KEY_TILES - 1, -1, -1) if m - d >= 0]
        key_rows = [slice((m - d) * A_QBLOCK, (m - d + 1) * A_QBLOCK) for d in dists]
        tile_max = []
        for t, (d, rows) in enumerate(zip(dists, key_rows)):
            logits = _dot_nt(qs, qkva_ref[rows, kcols]) + bias_a_ref[j, d]
            s_ref[slot, t] = logits
            tile_max.append(jnp.max(logits, axis=-1, keepdims=True))
        return functools.reduce(jnp.maximum, tile_max), key_rows

    def mixer_a_finish(slot, j, m, staged):
        mx, key_rows = staged
        acc = None
        for t, rows in enumerate(key_rows):
            pv = jnp.dot(jnp.exp2(s_ref[slot, t] - mx).astype(BF16),
                         vext_a_ref[rows, 2 * j * LANES:2 * (j + 1) * LANES],
                         preferred_element_type=F32)
            acc = pv if acc is None else acc + pv
        o = acc[:, :LANES] / acc[:, LANES:]
        ya_ref[m * A_QBLOCK:(m + 1) * A_QBLOCK, j * LANES:(j + 1) * LANES] = jnp.where(
            low, o[:A_QBLOCK], o[A_QBLOCK:]).astype(BF16)

    def mixer_b_scores(slot, m):
        krows = slice(max(m - 1, 0) * B_QBLOCK, (m + 1) * B_QBLOCK)
        width = krows.stop - krows.start
        qs = jnp.concatenate([qkvb_ref[m * B_QBLOCK:(m + 1) * B_QBLOCK, c * LANES:(c + 1) * LANES]
                              for c in range(B_GROUP)], axis=0)
        k = qkvb_ref[krows, B_Q_WIDTH:B_Q_WIDTH + LANES]
        zero = jnp.zeros_like(k)
        maxima = []
        for g in range(B_KV_HEADS):
            kg = jnp.where(low, k, zero) if g == 0 else jnp.where(low, zero, k)
            logits = _dot_nt(qs, kg) + bias_b_ref[g, :, B_WINDOW - width:]
            s_ref[slot, g, :, :width] = logits
            maxima.append(jnp.maximum(jnp.max(logits, axis=-1, keepdims=True), sink[g]))
        return maxima

    def mixer_b_finish(slot, m, staged):
        qrows = slice(m * B_QBLOCK, (m + 1) * B_QBLOCK)
        krows = slice(max(m - 1, 0) * B_QBLOCK, (m + 1) * B_QBLOCK)
        width = krows.stop - krows.start
        outs = []
        for g, mx in enumerate(staged):
            p = jnp.exp2(s_ref[slot, g, :, :width] - jnp.concatenate([mx] * (width // LANES), axis=1))
            pv = jnp.dot(p.astype(BF16), vext_b_ref[krows, :], preferred_element_type=F32)
            outs.append(pv[:, :LANES] / (pv[:, LANES:] + jnp.exp2(sink[g] - mx)))
        merged = jnp.where(low, outs[0], outs[1]).astype(BF16)
        for c in range(B_GROUP):
            yb_ref[qrows, c * LANES:(c + 1) * LANES] = merged[c * B_QBLOCK:(c + 1) * B_QBLOCK, :]

    for j in range(n_pairs):
        vcols = slice(2 * A_WIDTH + j * LANES, 2 * A_WIDTH + (j + 1) * LANES)
        vext_a_ref[:, 2 * j * LANES:(2 * j + 1) * LANES] = qkva_ref[:, vcols]
        vext_a_ref[:, (2 * j + 1) * LANES:2 * (j + 1) * LANES] = ones
    vext_b_ref[:, :LANES] = qkvb_ref[:, B_Q_WIDTH + LANES:]
    vext_b_ref[:, LANES:] = ones
    sink = [jnp.concatenate([jnp.full((B_QBLOCK, LANES), sinks_ref[g * B_GROUP + c], F32)
                             for c in range(B_GROUP)], axis=0) for g in range(B_KV_HEADS)]

    b_blocks_per_pair = seq // B_QBLOCK // n_pairs
    blocks = []
    for j in range(n_pairs):
        blocks += [(functools.partial(mixer_a_scores, j=j, m=m),
                    functools.partial(mixer_a_finish, j=j, m=m)) for m in range(seq // A_QBLOCK)]
        blocks += [(functools.partial(mixer_b_scores, m=m), functools.partial(mixer_b_finish, m=m))
                   for m in range(j * b_blocks_per_pair, (j + 1) * b_blocks_per_pair)]
    staged = blocks[0][0](slot=0)
    for n, (_, finish) in enumerate(blocks):
        next_staged = blocks[n + 1][0](slot=(n + 1) % 2) if n + 1 < len(blocks) else None
        finish(slot=n % 2, staged=staged)
        staged = next_staged


def _resident(shape):
    return pl.BlockSpec(shape, lambda *_: (0,) * len(shape), pipeline_mode=pl.Buffered(1))


def _bias_a_table_index():
    u = np.arange(2 * A_QBLOCK)
    key_minus_query = np.where(u <= A_QBLOCK, u, u - 2 * A_QBLOCK)
    rel = np.arange(A_KEY_TILES)[:, None] * A_QBLOCK - key_minus_query[None, :]
    return np.clip(rel, -(CHUNK - 1), MAX_REL) + (CHUNK - 1)


def _bias_b_table():
    slopes = np.array([2.0 ** (-8.0 * (h + 1) / B_Q_HEADS) for h in range(B_Q_HEADS)], np.float32)
    i = np.arange(B_QBLOCK)[:, None]
    j = np.arange(B_WINDOW)[None, :]
    rel = i - j + B_QBLOCK
    chunk_diff = i // CHUNK - j // CHUNK + B_QBLOCK // CHUNK
    vis = (chunk_diff >= 0) & (chunk_diff <= B_PREV_CHUNKS)
    dist = np.abs(rel).astype(np.float32)
    out = np.empty((B_KV_HEADS, B_GROUP * B_QBLOCK, B_WINDOW), np.float32)
    for g in range(B_KV_HEADS):
        for c in range(B_GROUP):
            out[g, c * B_QBLOCK:(c + 1) * B_QBLOCK] = np.where(
                vis, -slopes[g * B_GROUP + c] * dist * LOG2_E, NEG_INF)
    return out


def kernel(x, ffn1_norm, ffn1_w_gate, ffn1_w_up, ffn1_w_down, mix_norm, w_in, rel_bias, sinks,
           w_proj_a, w_proj_b, w_out, ffn2_norm, ffn2_w_gate, ffn2_w_up, ffn2_w_down, final_norm):
    batch, seq, d = x.shape
    rows = batch * seq
    assert d == D_MODEL and rows % ROW_TILE == 0 and seq % A_QBLOCK == 0
    n_row_tiles = rows // ROW_TILE
    scale = LOG2_E / np.sqrt(D_HEAD)

    b_perm = np.concatenate([np.arange(h * D_HEAD, (h + 1) * D_HEAD)
                             for c in range(B_GROUP) for h in (c, B_GROUP + c)])
    cuts = np.cumsum([A_WIDTH, A_WIDTH, A_WIDTH, B_Q_WIDTH, B_KV_WIDTH, B_KV_WIDTH, D_MODEL])
    wqa, wka, wva, wqb, wkb, wvb, wga, wgb = jnp.split(w_in, cuts, axis=1)
    w_qkva = jnp.concatenate([wqa * scale, wka, wva], axis=1).astype(BF16)
    w_qkvb = jnp.concatenate([wqb[:, b_perm] * scale, wkb, wvb], axis=1).astype(BF16)
    w_gates = jnp.concatenate([wga, wgb], axis=1).astype(BF16)
    w_pb = w_proj_b[b_perm, :].astype(BF16)
    vec = lambda g: g.reshape(1, D_MODEL).astype(F32)

    dense_params = pltpu.CompilerParams(dimension_semantics=("arbitrary",),
                                        vmem_limit_bytes=VMEM_LIMIT_BYTES)

    stage0_tile = lambda width, col=0: pl.BlockSpec(
        (ROW_TILE, width), lambda i: (jnp.minimum(i, n_row_tiles - 1), col))
    stage1_tile = lambda width: pl.BlockSpec(
        (ROW_TILE, width), lambda i: (jnp.maximum(i - 1, 0), 0))
    x1, qkva, qkvb, gates = pl.pallas_call(
        _ffn_in_kernel,
        grid=(n_row_tiles + 1,),
        in_specs=[stage0_tile(D_MODEL), _resident((1, D_MODEL)),
                  _resident((D_MODEL, D_FF)), _resident((D_MODEL, D_FF)),
                  _resident((D_FF, D_MODEL)), _resident((1, D_MODEL)),
                  _resident((D_MODEL, QKV_A_WIDTH)), _resident((D_MODEL, QKV_B_WIDTH)),
                  _resident((D_MODEL, GATES_WIDTH))],
        out_specs=[stage0_tile(D_MODEL), stage1_tile(QKV_A_WIDTH), stage1_tile(QKV_B_WIDTH),
                   stage1_tile(GATES_WIDTH)],
        out_shape=[jax.ShapeDtypeStruct((rows, D_MODEL), F32),
                   jax.ShapeDtypeStruct((rows, QKV_A_WIDTH), BF16),
                   jax.ShapeDtypeStruct((rows, QKV_B_WIDTH), BF16),
                   jax.ShapeDtypeStruct((rows, GATES_WIDTH), BF16)],
        scratch_shapes=[pltpu.VMEM((ROW_TILE, D_FF), BF16), pltpu.VMEM((ROW_TILE, D_MODEL), BF16)],
        compiler_params=dense_params,
        name="ffn_in",
    )(x.reshape(rows, D_MODEL), vec(ffn1_norm), ffn1_w_gate.astype(BF16),
      ffn1_w_up.astype(BF16), ffn1_w_down.astype(BF16), vec(mix_norm), w_qkva, w_qkvb, w_gates)

    tab_a = (rel_bias.astype(F32) * LOG2_E)[:, _bias_a_table_index()][:, :, None, :]
    n_pairs = A_WIDTH // LANES
    batch_block = lambda width: pl.BlockSpec((None, seq, width), lambda b: (b, 0, 0))
    ya, yb = pl.pallas_call(
        _attn_kernel,
        grid=(batch,),
        in_specs=[pl.BlockSpec(memory_space=pltpu.SMEM),
                  batch_block(QKV_A_WIDTH), batch_block(QKV_B_WIDTH),
                  _resident((A_HEADS, A_KEY_TILES, 1, 2 * A_QBLOCK)),
                  _resident((B_KV_HEADS, B_GROUP * B_QBLOCK, B_WINDOW))],
        out_specs=[batch_block(A_WIDTH), batch_block(B_Q_WIDTH)],
        out_shape=[jax.ShapeDtypeStruct((batch, seq, A_WIDTH), BF16),
                   jax.ShapeDtypeStruct((batch, seq, B_Q_WIDTH), BF16)],
        scratch_shapes=[pltpu.VMEM((n_pairs, A_KEY_TILES, 2 * A_QBLOCK, A_QBLOCK), F32),
                        pltpu.VMEM((seq, 2 * A_WIDTH), BF16),
                        pltpu.VMEM((seq, 2 * LANES), BF16),
                        pltpu.VMEM((2, A_KEY_TILES, 2 * A_QBLOCK, A_QBLOCK), F32)],
        compiler_params=pltpu.CompilerParams(dimension_semantics=("arbitrary",),
                                             vmem_limit_bytes=VMEM_LIMIT_BYTES),
        name="attn",
    )(sinks.astype(F32) * LOG2_E, qkva.reshape(batch, seq, QKV_A_WIDTH),
      qkvb.reshape(batch, seq, QKV_B_WIDTH), tab_a, jnp.asarray(_bias_b_table()))

    out = pl.pallas_call(
        _ffn_out_kernel,
        grid=(n_row_tiles + 1,),
        in_specs=[stage0_tile(D_MODEL), stage0_tile(A_WIDTH), stage0_tile(B_Q_WIDTH),
                  stage0_tile(D_MODEL, col=0), stage0_tile(D_MODEL, col=1),
                  _resident((A_WIDTH, D_MODEL)), _resident((B_Q_WIDTH, D_MODEL)),
                  _resident((D_MODEL, D_MODEL)), _resident((1, D_MODEL)),
                  _resident((D_MODEL, D_FF)), _resident((D_MODEL, D_FF)),
                  _resident((D_FF, D_MODEL)), _resident((1, D_MODEL))],
        out_specs=stage1_tile(D_MODEL),
        out_shape=jax.ShapeDtypeStruct((rows, D_MODEL), F32),
        scratch_shapes=[pltpu.VMEM((ROW_TILE, D_FF), BF16), pltpu.VMEM((ROW_TILE, D_MODEL), F32),
                        pltpu.VMEM((ROW_TILE, D_MODEL), BF16), pltpu.VMEM((ROW_TILE, D_MODEL), BF16)],
        compiler_params=dense_params,
        name="ffn_out",
    )(x1, ya.reshape(rows, A_WIDTH), yb.reshape(rows, B_Q_WIDTH), gates, gates,
      w_proj_a.astype(BF16), w_pb, w_out.astype(BF16), vec(ffn2_norm),
      ffn2_w_gate.astype(BF16), ffn2_w_up.astype(BF16), ffn2_w_down.astype(BF16),
      vec(final_norm))
    return out.reshape(batch, seq, D_MODEL)
```

```python
import functools

import numpy as np
import jax
import jax.numpy as jnp
from jax import lax
from jax.experimental import pallas as pl
from jax.experimental.pallas import tpu as pltpu

F32 = jnp.float32
BF16 = jnp.bfloat16

D_MODEL = 1024
D_FF = 2816
CHUNK = 64
D_HEAD = 64
A_HEADS = 8
A_PREV_CHUNKS = 8
MAX_REL = 128
B_Q_HEADS = 8
B_KV_HEADS = 2
B_GROUP = B_Q_HEADS // B_KV_HEADS
B_PREV_CHUNKS = 2
A_WIDTH = A_HEADS * D_HEAD
B_Q_WIDTH = B_Q_HEADS * D_HEAD
B_KV_WIDTH = B_KV_HEADS * D_HEAD
QKV_A_WIDTH = 3 * A_WIDTH
QKV_B_WIDTH = B_Q_WIDTH + 2 * B_KV_WIDTH
GATES_WIDTH = 2 * D_MODEL
EPS = 1e-6
NEG_INF = -1e30
LOG2_E = float(np.log2(np.e))

LANES = 128
BF16_SUBLANES = 16
MXU_DIM = 256
VMEM_LIMIT_BYTES = 56 * 1024 * 1024

ROW_TILE = 512
FF_CHUNK = MXU_DIM
A_QBLOCK = 4 * CHUNK
A_KEY_TILES = A_PREV_CHUNKS * CHUNK // A_QBLOCK + 1
B_QBLOCK = 2 * CHUNK
B_WINDOW = 2 * B_QBLOCK
assert B_GROUP * B_QBLOCK == 2 * A_QBLOCK and B_WINDOW == A_QBLOCK and B_KV_HEADS <= A_KEY_TILES


def _rms_norm(x, gain):
    return x * lax.rsqrt(jnp.mean(x * x, axis=-1, keepdims=True) + EPS) * gain


def _swiglu(h, wg_ref, wu_ref, wd_ref, hid_ref, between_chunks=None):
    between_chunks = between_chunks or {}
    for c in range(D_FF // FF_CHUNK):
        cols = slice(c * FF_CHUNK, (c + 1) * FF_CHUNK)
        g = jnp.dot(h, wg_ref[:, cols], preferred_element_type=F32)
        u = jnp.dot(h, wu_ref[:, cols], preferred_element_type=F32)
        hid_ref[:, cols] = (g * jax.nn.sigmoid(g) * u).astype(BF16)
        if c in between_chunks:
            between_chunks[c]()
    return jnp.dot(hid_ref[...], wd_ref[...], preferred_element_type=F32)


def _ffn_in_kernel(x_ref, g1_ref, wg_ref, wu_ref, wd_ref, gm_ref, wa_ref, wb_ref, wgt_ref,
                   wg2_ref, wu2_ref, wd2_ref,
                   x1_ref, qkva_ref, qkvb_ref, gates_ref, wg2_bf16_ref, wu2_bf16_ref, wd2_bf16_ref,
                   hid_ref, h2_ref):
    i = pl.program_id(0)
    last = pl.num_programs(0) - 1

    def cast_ffn2_weight_blocks():
        for src, dst in ((wg2_ref, wg2_bf16_ref), (wu2_ref, wu2_bf16_ref),
                         (wd2_ref, wd2_bf16_ref)):
            dst[...] = src[...].astype(BF16)

    def ffn_stage():
        x = x_ref[...]
        h = _rms_norm(x, g1_ref[...]).astype(BF16)
        x1 = x + 0.5 * _swiglu(h, wg_ref, wu_ref, wd_ref, hid_ref)
        x1_ref[...] = x1
        h2_ref[...] = _rms_norm(x1, gm_ref[...]).astype(BF16)

    def projection_stage():
        h2 = h2_ref[...]
        qkva_ref[...] = jnp.dot(h2, wa_ref[...], preferred_element_type=F32).astype(BF16)
        qkvb_ref[...] = jnp.dot(h2, wb_ref[...], preferred_element_type=F32).astype(BF16)
        gates_ref[...] = jnp.dot(h2, wgt_ref[...], preferred_element_type=F32).astype(BF16)

    @pl.when(i == 0)
    def _first():
        cast_ffn2_weight_blocks()
        ffn_stage()

    @pl.when((i > 0) & (i < last))
    def _steady():
        cast_ffn2_weight_blocks()
        projection_stage()
        ffn_stage()

    @pl.when(i == last)
    def _last():
        cast_ffn2_weight_blocks()
        projection_stage()


def _ffn_out_kernel(x1_ref, ya_ref, yb_ref, ga_ref, gb_ref, wpa_ref, wpb_ref, wo_ref,
                    g2_ref, wg_ref, wu_ref, wd_ref, gf_ref, out_ref, hid_ref, x2_ref, h_ref,
                    mg_ref):
    i = pl.program_id(0)
    last = pl.num_programs(0) - 1
    merged_tile = {}

    def gate_columns(c):
        cols = slice(c * MXU_DIM, (c + 1) * MXU_DIM)
        pa = jnp.dot(ya_ref[...], wpa_ref[:, cols], preferred_element_type=F32)
        pb = jnp.dot(yb_ref[...], wpb_ref[:, cols], preferred_element_type=F32)
        mg_ref[:, cols] = (jax.nn.sigmoid(ga_ref[:, cols].astype(F32)) * pa
                           + jax.nn.sigmoid(gb_ref[:, cols].astype(F32)) * pb).astype(BF16)

    def mix_in():
        x2 = x1_ref[...] + jnp.dot(mg_ref[...], wo_ref[...], preferred_element_type=F32)
        merged_tile["x2"] = x2
        merged_tile["h"] = _rms_norm(x2, g2_ref[...]).astype(BF16)

    def store_merged_tile():
        x2_ref[...] = merged_tile["x2"]
        h_ref[...] = merged_tile["h"]

    merge_pieces = [functools.partial(gate_columns, c) for c in range(D_MODEL // MXU_DIM)]
    merge_pieces.append(mix_in)

    def ffn_stage(between_chunks=None):
        x3 = x2_ref[...] + 0.5 * _swiglu(h_ref[...], wg_ref, wu_ref, wd_ref, hid_ref,
                                         between_chunks)
        out_ref[...] = _rms_norm(x3, gf_ref[...])

    @pl.when(i == 0)
    def _first():
        for piece in merge_pieces:
            piece()
        store_merged_tile()

    @pl.when((i > 0) & (i < last))
    def _steady():
        ffn_stage(dict(zip((0, 1, 2, 3, 5), merge_pieces)))
        store_merged_tile()

    @pl.when(i == last)
    def _last():
        ffn_stage()


def _dot_nt(a, b):
    return lax.dot_general(a, b, (((1,), (1,)), ((), ())), preferred_element_type=F32)


def _attn_kernel(sinks_ref, qkva_ref, qkvb_ref, tab_ref, bias_b_ref, ya_ref, yb_ref,
                 bias_a_ref, vext_a_ref, vext_b_ref, s_ref):
    seq = qkva_ref.shape[0]
    n_pairs = A_WIDTH // LANES
    low = lax.broadcasted_iota(jnp.int32, (1, LANES), 1) < D_HEAD
    ones = jnp.ones((seq, LANES), BF16)

    @pl.when(pl.program_id(0) == 0)
    def _build_bias_a():
        qc = lax.broadcasted_iota(jnp.int32, (A_QBLOCK, A_QBLOCK), 0) // CHUNK
        kc = lax.broadcasted_iota(jnp.int32, (A_QBLOCK, A_QBLOCK), 1) // CHUNK
        for h in range(A_HEADS):
            for d in range(A_KEY_TILES):
                row = jnp.broadcast_to(tab_ref[h, d], (A_QBLOCK, 2 * A_QBLOCK))
                toeplitz = pltpu.roll(row, 0, 1, stride=1, stride_axis=0)[:, :A_QBLOCK]
                chunk_diff = d * (A_QBLOCK // CHUNK) + qc - kc
                visible = (chunk_diff >= 0) & (chunk_diff <= A_PREV_CHUNKS)
                bias_a_ref[h // 2, d, (h % 2) * A_QBLOCK:(h % 2 + 1) * A_QBLOCK, :] = jnp.where(
                    visible, toeplitz, NEG_INF)

    def mixer_a_scores(slot, j, m):
        q = qkva_ref[m * A_QBLOCK:(m + 1) * A_QBLOCK, j * LANES:(j + 1) * LANES]
        zero = jnp.zeros_like(q)
        qs = jnp.concatenate([jnp.where(low, q, zero), jnp.where(low, zero, q)], axis=0)
        kcols = slice(A_WIDTH + j * LANES, A_WIDTH + (j + 1) * LANES)
        dists = [d for d in range(A_KEY_TILES - 1, -1, -1) if m - d >= 0]
        key_rows = [slice((m - d) * A_QBLOCK, (m - d + 1) * A_QBLOCK) for d in dists]
        tile_max = []
        for t, (d, rows) in enumerate(zip(dists, key_rows)):
            logits = _dot_nt(qs, qkva_ref[rows, kcols]) + bias_a_ref[j, d]
            s_ref[slot, t] = logits
            tile_max.append(jnp.max(logits, axis=-1, keepdims=True))
        return functools.reduce(jnp.maximum, tile_max), key_rows

    def mixer_a_finish(slot, j, m, staged):
        mx, key_rows = staged
        acc = None
        for t, rows in enumerate(key_rows):
            pv = jnp.dot(jnp.exp2(s_ref[slot, t] - mx).astype(BF16),
                         vext_a_ref[rows, 2 * j * LANES:2 * (j + 1) * LANES],
                         preferred_element_type=F32)
            acc = pv if acc is None else acc + pv
        o = acc[:, :LANES] / acc[:, LANES:]
        ya_ref[m * A_QBLOCK:(m + 1) * A_QBLOCK, j * LANES:(j + 1) * LANES] = jnp.where(
            low, o[:A_QBLOCK], o[A_QBLOCK:]).astype(BF16)

    def mixer_b_scores(slot, m):
        krows = slice(max(m - 1, 0) * B_QBLOCK, (m + 1) * B_QBLOCK)
        width = krows.stop - krows.start
        qs = jnp.concatenate([qkvb_ref[m * B_QBLOCK:(m + 1) * B_QBLOCK, c * LANES:(c + 1) * LANES]
                              for c in range(B_GROUP)], axis=0)
        k = qkvb_ref[krows, B_Q_WIDTH:B_Q_WIDTH + LANES]
        zero = jnp.zeros_like(k)
        maxima = []
        for g in range(B_KV_HEADS):
            kg = jnp.where(low, k, zero) if g == 0 else jnp.where(low, zero, k)
            logits = _dot_nt(qs, kg) + bias_b_ref[g, :, B_WINDOW - width:]
            s_ref[slot, g, :, :width] = logits
            maxima.append(jnp.maximum(jnp.max(logits, axis=-1, keepdims=True), sink[g]))
        return maxima

    def mixer_b_finish(slot, m, staged):
        qrows = slice(m * B_QBLOCK, (m + 1) * B_QBLOCK)
        krows = slice(max(m - 1, 0) * B_QBLOCK, (m + 1) * B_QBLOCK)
        width = krows.stop - krows.start
        outs = []
        for g, mx in enumerate(staged):
            p = jnp.exp2(s_ref[slot, g, :, :width] - jnp.concatenate([mx] * (width // LANES), axis=1))
            pv = jnp.dot(p.astype(BF16), vext_b_ref[krows, :], preferred_element_type=F32)
            outs.append(pv[:, :LANES] / (pv[:, LANES:] + jnp.exp2(sink[g] - mx)))
        merged = jnp.where(low, outs[0], outs[1]).astype(BF16)
        for c in range(B_GROUP):
            yb_ref[qrows, c * LANES:(c + 1) * LANES] = merged[c * B_QBLOCK:(c + 1) * B_QBLOCK, :]

    for j in range(n_pairs):
        vcols = slice(2 * A_WIDTH + j * LANES, 2 * A_WIDTH + (j + 1) * LANES)
        vext_a_ref[:, 2 * j * LANES:(2 * j + 1) * LANES] = qkva_ref[:, vcols]
        vext_a_ref[:, (2 * j + 1) * LANES:2 * (j + 1) * LANES] = ones
    vext_b_ref[:, :LANES] = qkvb_ref[:, B_Q_WIDTH + LANES:]
    vext_b_ref[:, LANES:] = ones
    sink = [jnp.concatenate([jnp.full((B_QBLOCK, LANES), sinks_ref[g * B_GROUP + c], F32)
                             for c in range(B_GROUP)], axis=0) for g in range(B_KV_HEADS)]

    b_blocks_per_pair = seq // B_QBLOCK // n_pairs
    blocks = []
    for j in range(n_pairs):
        blocks += [(functools.partial(mixer_a_scores, j=j, m=m),
                    functools.partial(mixer_a_finish, j=j, m=m)) for m in range(seq // A_QBLOCK)]
        blocks += [(functools.partial(mixer_b_scores, m=m), functools.partial(mixer_b_finish, m=m))
                   for m in range(j * b_blocks_per_pair, (j + 1) * b_blocks_per_pair)]
    staged = blocks[0][0](slot=0)
    for n, (_, finish) in enumerate(blocks):
        next_staged = blocks[n + 1][0](slot=(n + 1) % 2) if n + 1 < len(blocks) else None
        finish(slot=n % 2, staged=staged)
        staged = next_staged


def _resident(shape):
    return pl.BlockSpec(shape, lambda *_: (0,) * len(shape), pipeline_mode=pl.Buffered(1))


def _bias_a_table_index():
    u = np.arange(2 * A_QBLOCK)
    key_minus_query = np.where(u <= A_QBLOCK, u, u - 2 * A_QBLOCK)
    rel = np.arange(A_KEY_TILES)[:, None] * A_QBLOCK - key_minus_query[None, :]
    return np.clip(rel, -(CHUNK - 1), MAX_REL) + (CHUNK - 1)


def _bias_b_table():
    slopes = np.array([2.0 ** (-8.0 * (h + 1) / B_Q_HEADS) for h in range(B_Q_HEADS)], np.float32)
    i = np.arange(B_QBLOCK)[:, None]
    j = np.arange(B_WINDOW)[None, :]
    rel = i - j + B_QBLOCK
    chunk_diff = i // CHUNK - j // CHUNK + B_QBLOCK // CHUNK
    vis = (chunk_diff >= 0) & (chunk_diff <= B_PREV_CHUNKS)
    dist = np.abs(rel).astype(np.float32)
    out = np.empty((B_KV_HEADS, B_GROUP * B_QBLOCK, B_WINDOW), np.float32)
    for g in range(B_KV_HEADS):
        for c in range(B_GROUP):
            out[g, c * B_QBLOCK:(c + 1) * B_QBLOCK] = np.where(
                vis, -slopes[g * B_GROUP + c] * dist * LOG2_E, NEG_INF)
    return out


def kernel(x, ffn1_norm, ffn1_w_gate, ffn1_w_up, ffn1_w_down, mix_norm, w_in, rel_bias, sinks,
           w_proj_a, w_proj_b, w_out, ffn2_norm, ffn2_w_gate, ffn2_w_up, ffn2_w_down, final_norm):
    batch, seq, d = x.shape
    rows = batch * seq
    assert d == D_MODEL and rows % ROW_TILE == 0 and seq % A_QBLOCK == 0
    n_row_tiles = rows // ROW_TILE
    scale = LOG2_E / np.sqrt(D_HEAD)

    b_perm = np.concatenate([np.arange(h * D_HEAD, (h + 1) * D_HEAD)
                             for c in range(B_GROUP) for h in (c, B_GROUP + c)])
    cuts = np.cumsum([A_WIDTH, A_WIDTH, A_WIDTH, B_Q_WIDTH, B_KV_WIDTH, B_KV_WIDTH, D_MODEL])
    wqa, wka, wva, wqb, wkb, wvb, wga, wgb = jnp.split(w_in, cuts, axis=1)
    w_qkva = jnp.concatenate([wqa * scale, wka, wva], axis=1).astype(BF16)
    w_qkvb = jnp.concatenate([wqb[:, b_perm] * scale, wkb, wvb], axis=1).astype(BF16)
    w_gates = jnp.concatenate([wga, wgb], axis=1).astype(BF16)
    w_pb = w_proj_b[b_perm, :].astype(BF16)
    vec = lambda g: g.reshape(1, D_MODEL).astype(F32)

    dense_params = pltpu.CompilerParams(dimension_semantics=("arbitrary",),
                                        vmem_limit_bytes=VMEM_LIMIT_BYTES)

    stage0_tile = lambda width, col=0: pl.BlockSpec(
        (ROW_TILE, width), lambda i: (jnp.minimum(i, n_row_tiles - 1), col))
    stage1_tile = lambda width: pl.BlockSpec(
        (ROW_TILE, width), lambda i: (jnp.maximum(i - 1, 0), 0))
    def cast_block(weight):
        n_rows, width = weight.shape
        block_rows = next(r for r in range(BF16_SUBLANES, n_rows + 1, BF16_SUBLANES)
                          if n_rows % r == 0 and n_rows // r <= n_row_tiles)
        return pl.BlockSpec((block_rows, width),
                            lambda i: (jnp.minimum(i, n_rows // block_rows - 1), 0))
    ffn2_weights = (ffn2_w_gate, ffn2_w_up, ffn2_w_down)
    x1, qkva, qkvb, gates, wg2, wu2, wd2 = pl.pallas_call(
        _ffn_in_kernel,
        grid=(n_row_tiles + 1,),
        in_specs=[stage0_tile(D_MODEL), _resident((1, D_MODEL)),
                  _resident((D_MODEL, D_FF)), _resident((D_MODEL, D_FF)),
                  _resident((D_FF, D_MODEL)), _resident((1, D_MODEL)),
                  _resident((D_MODEL, QKV_A_WIDTH)), _resident((D_MODEL, QKV_B_WIDTH)),
                  _resident((D_MODEL, GATES_WIDTH))] + [cast_block(w) for w in ffn2_weights],
        out_specs=[stage0_tile(D_MODEL), stage1_tile(QKV_A_WIDTH), stage1_tile(QKV_B_WIDTH),
                   stage1_tile(GATES_WIDTH)] + [cast_block(w) for w in ffn2_weights],
        out_shape=[jax.ShapeDtypeStruct((rows, D_MODEL), F32),
                   jax.ShapeDtypeStruct((rows, QKV_A_WIDTH), BF16),
                   jax.ShapeDtypeStruct((rows, QKV_B_WIDTH), BF16),
                   jax.ShapeDtypeStruct((rows, GATES_WIDTH), BF16)]
                  + [jax.ShapeDtypeStruct(w.shape, BF16) for w in ffn2_weights],
        scratch_shapes=[pltpu.VMEM((ROW_TILE, D_FF), BF16), pltpu.VMEM((ROW_TILE, D_MODEL), BF16)],
        compiler_params=dense_params,
        name="ffn_in",
    )(x.reshape(rows, D_MODEL), vec(ffn1_norm), ffn1_w_gate.astype(BF16),
      ffn1_w_up.astype(BF16), ffn1_w_down.astype(BF16), vec(mix_norm), w_qkva, w_qkvb, w_gates,
      *ffn2_weights)

    tab_a = (rel_bias.astype(F32) * LOG2_E)[:, _bias_a_table_index()][:, :, None, :]
    n_pairs = A_WIDTH // LANES
    batch_block = lambda width: pl.BlockSpec((None, seq, width), lambda b: (b, 0, 0))
    ya, yb = pl.pallas_call(
        _attn_kernel,
        grid=(batch,),
        in_specs=[pl.BlockSpec(memory_space=pltpu.SMEM),
                  batch_block(QKV_A_WIDTH), batch_block(QKV_B_WIDTH),
                  _resident((A_HEADS, A_KEY_TILES, 1, 2 * A_QBLOCK)),
                  _resident((B_KV_HEADS, B_GROUP * B_QBLOCK, B_WINDOW))],
        out_specs=[batch_block(A_WIDTH), batch_block(B_Q_WIDTH)],
        out_shape=[jax.ShapeDtypeStruct((batch, seq, A_WIDTH), BF16),
                   jax.ShapeDtypeStruct((batch, seq, B_Q_WIDTH), BF16)],
        scratch_shapes=[pltpu.VMEM((n_pairs, A_KEY_TILES, 2 * A_QBLOCK, A_QBLOCK), F32),
                        pltpu.VMEM((seq, 2 * A_WIDTH), BF16),
                        pltpu.VMEM((seq, 2 * LANES), BF16),
                        pltpu.VMEM((2, A_KEY_TILES, 2 * A_QBLOCK, A_QBLOCK), F32)],
        compiler_params=pltpu.CompilerParams(dimension_semantics=("arbitrary",),
                                             vmem_limit_bytes=VMEM_LIMIT_BYTES),
        name="attn",
    )(sinks.astype(F32) * LOG2_E, qkva.reshape(batch, seq, QKV_A_WIDTH),
      qkvb.reshape(batch, seq, QKV_B_WIDTH), tab_a, jnp.asarray(_bias_b_table()))

    out = pl.pallas_call(
        _ffn_out_kernel,
        grid=(n_row_tiles + 1,),
        in_specs=[stage0_tile(D_MODEL), stage0_tile(A_WIDTH), stage0_tile(B_Q_WIDTH),
                  stage0_tile(D_MODEL, col=0), stage0_tile(D_MODEL, col=1),
                  _resident((A_WIDTH, D_MODEL)), _resident((B_Q_WIDTH, D_MODEL)),
                  _resident((D_MODEL, D_MODEL)), _resident((1, D_MODEL)),
                  _resident((D_MODEL, D_FF)), _resident((D_MODEL, D_FF)),
                  _resident((D_FF, D_MODEL)), _resident((1, D_MODEL))],
        out_specs=stage1_tile(D_MODEL),
        out_shape=jax.ShapeDtypeStruct((rows, D_MODEL), F32),
        scratch_shapes=[pltpu.VMEM((ROW_TILE, D_FF), BF16), pltpu.VMEM((ROW_TILE, D_MODEL), F32),
                        pltpu.VMEM((ROW_TILE, D_MODEL), BF16), pltpu.VMEM((ROW_TILE, D_MODEL), BF16)],
        compiler_params=dense_params,
        name="ffn_out",
    )(x1, ya.reshape(rows, A_WIDTH), yb.reshape(rows, B_Q_WIDTH), gates, gates,
      w_proj_a.astype(BF16), w_pb, w_out.astype(BF16), vec(ffn2_norm),
      wg2, wu2, wd2, vec(final_norm))
    return out.reshape(batch, seq, D_MODEL)
```

```python
import functools

import numpy as np
import jax
import jax.numpy as jnp
from jax import lax
from jax.experimental import pallas as pl
from jax.experimental.pallas import tpu as pltpu

F32 = jnp.float32
BF16 = jnp.bfloat16

D_MODEL = 1024
D_FF = 2816
CHUNK = 64
D_HEAD = 64
A_HEADS = 8
A_PREV_CHUNKS = 8
MAX_REL = 128
B_Q_HEADS = 8
B_KV_HEADS = 2
B_GROUP = B_Q_HEADS // B_KV_HEADS
B_PREV_CHUNKS = 2
A_WIDTH = A_HEADS * D_HEAD
B_Q_WIDTH = B_Q_HEADS * D_HEAD
B_KV_WIDTH = B_KV_HEADS * D_HEAD
QKV_A_WIDTH = 3 * A_WIDTH
QKV_B_WIDTH = B_Q_WIDTH + 2 * B_KV_WIDTH
GATES_WIDTH = 2 * D_MODEL
EPS = 1e-6
NEG_INF = -1e30
LOG2_E = float(np.log2(np.e))

LANES = 128
MXU_DIM = 256
VMEM_LIMIT_BYTES = 56 * 1024 * 1024

ROW_TILE = 512
FF_CHUNK = MXU_DIM
A_QBLOCK = 4 * CHUNK
A_KEY_TILES = A_PREV_CHUNKS * CHUNK // A_QBLOCK + 1
B_QBLOCK = 2 * CHUNK
B_WINDOW = 2 * B_QBLOCK
assert B_GROUP * B_QBLOCK == 2 * A_QBLOCK and B_WINDOW == A_QBLOCK and B_KV_HEADS <= A_KEY_TILES


def _rms_norm(x, gain):
    return x * lax.rsqrt(jnp.mean(x * x, axis=-1, keepdims=True) + EPS) * gain


def _swiglu(h, wg_ref, wu_ref, wd_ref, hid_ref, between_chunks=None):
    between_chunks = between_chunks or {}
    for c in range(D_FF // FF_CHUNK):
        cols = slice(c * FF_CHUNK, (c + 1) * FF_CHUNK)
        g = jnp.dot(h, wg_ref[:, cols], preferred_element_type=F32)
        u = jnp.dot(h, wu_ref[:, cols], preferred_element_type=F32)
        hid_ref[:, cols] = (g * jax.nn.sigmoid(g) * u).astype(BF16)
        if c in between_chunks:
            between_chunks[c]()
    return jnp.dot(hid_ref[...], wd_ref[...], preferred_element_type=F32)


def _ffn_in_kernel(x_ref, g1_ref, wg_ref, wu_ref, wd_ref, gm_ref, win_ref,
                   x1_ref, qkva_ref, qkvb_ref, gates_ref, hid_ref, h2_ref):
    i = pl.program_id(0)
    last = pl.num_programs(0) - 1

    def ffn_stage():
        x = x_ref[...]
        h = _rms_norm(x, g1_ref[...]).astype(BF16)
        x1 = x + 0.5 * _swiglu(h, wg_ref, wu_ref, wd_ref, hid_ref)
        x1_ref[...] = x1
        h2_ref[...] = _rms_norm(x1, gm_ref[...]).astype(BF16)

    def projection_stage():
        h2 = h2_ref[...]
        col = 0
        for o_ref in (qkva_ref, qkvb_ref, gates_ref):
            width = o_ref.shape[1]
            o_ref[...] = jnp.dot(h2, win_ref[:, col:col + width],
                                 preferred_element_type=F32).astype(BF16)
            col += width

    @pl.when(i == 0)
    def _first():
        ffn_stage()

    @pl.when((i > 0) & (i < last))
    def _steady():
        projection_stage()
        ffn_stage()

    @pl.when(i == last)
    def _last():
        projection_stage()


def _ffn_out_kernel(x1_ref, ya_ref, yb_ref, ga_ref, gb_ref, wpa_ref, wpb_ref, wo_ref,
                    g2_ref, wg_ref, wu_ref, wd_ref, gf_ref, out_ref, hid_ref, x2_ref, h_ref,
                    mg_ref):
    i = pl.program_id(0)
    last = pl.num_programs(0) - 1
    merged_tile = {}

    def gate_columns(c):
        cols = slice(c * MXU_DIM, (c + 1) * MXU_DIM)
        pa = jnp.dot(ya_ref[...], wpa_ref[:, cols], preferred_element_type=F32)
        pb = jnp.dot(yb_ref[...], wpb_ref[:, cols], preferred_element_type=F32)
        mg_ref[:, cols] = (jax.nn.sigmoid(ga_ref[:, cols].astype(F32)) * pa
                           + jax.nn.sigmoid(gb_ref[:, cols].astype(F32)) * pb).astype(BF16)

    def mix_in():
        x2 = x1_ref[...] + jnp.dot(mg_ref[...], wo_ref[...], preferred_element_type=F32)
        merged_tile["x2"] = x2
        merged_tile["h"] = _rms_norm(x2, g2_ref[...]).astype(BF16)

    def store_merged_tile():
        x2_ref[...] = merged_tile["x2"]
        h_ref[...] = merged_tile["h"]

    merge_pieces = [functools.partial(gate_columns, c) for c in range(D_MODEL // MXU_DIM)]
    merge_pieces.append(mix_in)

    def ffn_stage(between_chunks=None):
        x3 = x2_ref[...] + 0.5 * _swiglu(h_ref[...], wg_ref, wu_ref, wd_ref, hid_ref,
                                         between_chunks)
        out_ref[...] = _rms_norm(x3, gf_ref[...])

    @pl.when(i == 0)
    def _first():
        for piece in merge_pieces:
            piece()
        store_merged_tile()

    @pl.when((i > 0) & (i < last))
    def _steady():
        ffn_stage(dict(zip((0, 1, 2, 3, 5), merge_pieces)))
        store_merged_tile()

    @pl.when(i == last)
    def _last():
        ffn_stage()


def _dot_nt(a, b):
    return lax.dot_general(a, b, (((1,), (1,)), ((), ())), preferred_element_type=F32)


def _attn_kernel(sinks_ref, qkva_ref, qkvb_ref, tab_ref, bias_b_ref, ya_ref, yb_ref,
                 bias_a_ref, vext_a_ref, vext_b_ref, s_ref):
    seq = qkva_ref.shape[0]
    n_pairs = A_WIDTH // LANES
    low = lax.broadcasted_iota(jnp.int32, (1, LANES), 1) < D_HEAD
    ones = jnp.ones((seq, LANES), BF16)

    @pl.when(pl.program_id(0) == 0)
    def _build_bias_a():
        qc = lax.broadcasted_iota(jnp.int32, (A_QBLOCK, A_QBLOCK), 0) // CHUNK
        kc = lax.broadcasted_iota(jnp.int32, (A_QBLOCK, A_QBLOCK), 1) // CHUNK
        for h in range(A_HEADS):
            for d in range(A_KEY_TILES):
                row = jnp.broadcast_to(tab_ref[h, d], (A_QBLOCK, 2 * A_QBLOCK))
                toeplitz = pltpu.roll(row, 0, 1, stride=1, stride_axis=0)[:, :A_QBLOCK]
                chunk_diff = d * (A_QBLOCK // CHUNK) + qc - kc
                visible = (chunk_diff >= 0) & (chunk_diff <= A_PREV_CHUNKS)
                bias_a_ref[h // 2, d, (h % 2) * A_QBLOCK:(h % 2 + 1) * A_QBLOCK, :] = jnp.where(
                    visible, toeplitz, NEG_INF)

    def mixer_a_scores(slot, j, m):
        q = qkva_ref[m * A_QBLOCK:(m + 1) * A_QBLOCK, j * LANES:(j + 1) * LANES]
        zero = jnp.zeros_like(q)
        qs = jnp.concatenate([jnp.where(low, q, zero), jnp.where(low, zero, q)], axis=0)
        kcols = slice(A_WIDTH + j * LANES, A_WIDTH + (j + 1) * LANES)
        dists = [d for d in range(A_KEY_TILES - 1, -1, -1) if m - d >= 0]
        key_rows = [slice((m - d) * A_QBLOCK, (m - d + 1) * A_QBLOCK) for d in dists]
        tile_max = []
        for t, (d, rows) in enumerate(zip(dists, key_rows)):
            logits = _dot_nt(qs, qkva_ref[rows, kcols]) + bias_a_ref[j, d]
            s_ref[slot, t] = logits
            tile_max.append(jnp.max(logits, axis=-1, keepdims=True))
        return functools.reduce(jnp.maximum, tile_max), key_rows

    def mixer_a_finish(slot, j, m, staged):
        mx, key_rows = staged
        acc = None
        for t, rows in enumerate(key_rows):
            pv = jnp.dot(jnp.exp2(s_ref[slot, t] - mx).astype(BF16),
                         vext_a_ref[rows, 2 * j * LANES:2 * (j + 1) * LANES],
                         preferred_element_type=F32)
            acc = pv if acc is None else acc + pv
        o = acc[:, :LANES] / acc[:, LANES:]
        ya_ref[m * A_QBLOCK:(m + 1) * A_QBLOCK, j * LANES:(j + 1) * LANES] = jnp.where(
            low, o[:A_QBLOCK], o[A_QBLOCK:]).astype(BF16)

    def mixer_b_scores(slot, m):
        krows = slice(max(m - 1, 0) * B_QBLOCK, (m + 1) * B_QBLOCK)
        width = krows.stop - krows.start
        qs = jnp.concatenate([qkvb_ref[m * B_QBLOCK:(m + 1) * B_QBLOCK, c * LANES:(c + 1) * LANES]
                              for c in range(B_GROUP)], axis=0)
        k = qkvb_ref[krows, B_Q_WIDTH:B_Q_WIDTH + LANES]
        zero = jnp.zeros_like(k)
        maxima = []
        for g in range(B_KV_HEADS):
            kg = jnp.where(low, k, zero) if g == 0 else jnp.where(low, zero, k)
            logits = _dot_nt(qs, kg) + bias_b_ref[g, :, B_WINDOW - width:]
            s_ref[slot, g, :, :width] = logits
            maxima.append(jnp.maximum(jnp.max(logits, axis=-1, keepdims=True), sink[g]))
        return maxima

    def mixer_b_finish(slot, m, staged):
        qrows = slice(m * B_QBLOCK, (m + 1) * B_QBLOCK)
        krows = slice(max(m - 1, 0) * B_QBLOCK, (m + 1) * B_QBLOCK)
        width = krows.stop - krows.start
        outs = []
        for g, mx in enumerate(staged):
            p = jnp.exp2(s_ref[slot, g, :, :width] - jnp.concatenate([mx] * (width // LANES), axis=1))
            pv = jnp.dot(p.astype(BF16), vext_b_ref[krows, :], preferred_element_type=F32)
            outs.append(pv[:, :LANES] / (pv[:, LANES:] + jnp.exp2(sink[g] - mx)))
        merged = jnp.where(low, outs[0], outs[1]).astype(BF16)
        for c in range(B_GROUP):
            yb_ref[qrows, c * LANES:(c + 1) * LANES] = merged[c * B_QBLOCK:(c + 1) * B_QBLOCK, :]

    for j in range(n_pairs):
        vcols = slice(2 * A_WIDTH + j * LANES, 2 * A_WIDTH + (j + 1) * LANES)
        vext_a_ref[:, 2 * j * LANES:(2 * j + 1) * LANES] = qkva_ref[:, vcols]
        vext_a_ref[:, (2 * j + 1) * LANES:2 * (j + 1) * LANES] = ones
    vext_b_ref[:, :LANES] = qkvb_ref[:, B_Q_WIDTH + LANES:]
    vext_b_ref[:, LANES:] = ones
    sink = [jnp.concatenate([jnp.full((B_QBLOCK, LANES), sinks_ref[g * B_GROUP + c], F32)
                             for c in range(B_GROUP)], axis=0) for g in range(B_KV_HEADS)]

    b_blocks_per_pair = seq // B_QBLOCK // n_pairs
    blocks = []
    for j in range(n_pairs):
        blocks += [(functools.partial(mixer_a_scores, j=j, m=m),
                    functools.partial(mixer_a_finish, j=j, m=m)) for m in range(seq // A_QBLOCK)]
        blocks += [(functools.partial(mixer_b_scores, m=m), functools.partial(mixer_b_finish, m=m))
                   for m in range(j * b_blocks_per_pair, (j + 1) * b_blocks_per_pair)]
    staged = blocks[0][0](slot=0)
    for n, (_, finish) in enumerate(blocks):
        next_staged = blocks[n + 1][0](slot=(n + 1) % 2) if n + 1 < len(blocks) else None
        finish(slot=n % 2, staged=staged)
        staged = next_staged


def _resident(shape):
    return pl.BlockSpec(shape, lambda *_: (0,) * len(shape), pipeline_mode=pl.Buffered(1))


def _bias_a_table_index():
    u = np.arange(2 * A_QBLOCK)
    key_minus_query = np.where(u <= A_QBLOCK, u, u - 2 * A_QBLOCK)
    rel = np.arange(A_KEY_TILES)[:, None] * A_QBLOCK - key_minus_query[None, :]
    return np.clip(rel, -(CHUNK - 1), MAX_REL) + (CHUNK - 1)


def _bias_b_table():
    slopes = np.array([2.0 ** (-8.0 * (h + 1) / B_Q_HEADS) for h in range(B_Q_HEADS)], np.float32)
    i = np.arange(B_QBLOCK)[:, None]
    j = np.arange(B_WINDOW)[None, :]
    rel = i - j + B_QBLOCK
    chunk_diff = i // CHUNK - j // CHUNK + B_QBLOCK // CHUNK
    vis = (chunk_diff >= 0) & (chunk_diff <= B_PREV_CHUNKS)
    dist = np.abs(rel).astype(np.float32)
    out = np.empty((B_KV_HEADS, B_GROUP * B_QBLOCK, B_WINDOW), np.float32)
    for g in range(B_KV_HEADS):
        for c in range(B_GROUP):
            out[g, c * B_QBLOCK:(c + 1) * B_QBLOCK] = np.where(
                vis, -slopes[g * B_GROUP + c] * dist * LOG2_E, NEG_INF)
    return out


def kernel(x, ffn1_norm, ffn1_w_gate, ffn1_w_up, ffn1_w_down, mix_norm, w_in, rel_bias, sinks,
           w_proj_a, w_proj_b, w_out, ffn2_norm, ffn2_w_gate, ffn2_w_up, ffn2_w_down, final_norm):
    batch, seq, d = x.shape
    rows = batch * seq
    assert d == D_MODEL and rows % ROW_TILE == 0 and seq % A_QBLOCK == 0
    n_row_tiles = rows // ROW_TILE
    scale = LOG2_E / np.sqrt(D_HEAD)

    b_heads = [h for c in range(B_GROUP) for h in (c, B_GROUP + c)]
    qb0 = QKV_A_WIDTH
    w_in_cols = ([(0, A_WIDTH, scale), (A_WIDTH, QKV_A_WIDTH, 1.0)]
                 + [(qb0 + h * D_HEAD, qb0 + (h + 1) * D_HEAD, scale) for h in b_heads]
                 + [(qb0 + B_Q_WIDTH, w_in.shape[1], 1.0)])
    w_in_bf16 = jnp.concatenate([w_in[:, a:b] * s if s != 1.0 else w_in[:, a:b]
                                 for a, b, s in w_in_cols], axis=1).astype(BF16)
    w_pb = jnp.concatenate([w_proj_b[h * D_HEAD:(h + 1) * D_HEAD] for h in b_heads],
                           axis=0).astype(BF16)
    vec = lambda g: g.reshape(1, D_MODEL).astype(F32)

    dense_params = pltpu.CompilerParams(dimension_semantics=("arbitrary",),
                                        vmem_limit_bytes=VMEM_LIMIT_BYTES)

    stage0_tile = lambda width, col=0: pl.BlockSpec(
        (ROW_TILE, width), lambda i: (jnp.minimum(i, n_row_tiles - 1), col))
    stage1_tile = lambda width: pl.BlockSpec(
        (ROW_TILE, width), lambda i: (jnp.maximum(i - 1, 0), 0))
    x1, qkva, qkvb, gates = pl.pallas_call(
        _ffn_in_kernel,
        grid=(n_row_tiles + 1,),
        in_specs=[stage0_tile(D_MODEL), _resident((1, D_MODEL)),
                  _resident((D_MODEL, D_FF)), _resident((D_MODEL, D_FF)),
                  _resident((D_FF, D_MODEL)), _resident((1, D_MODEL)),
                  _resident(w_in_bf16.shape)],
        out_specs=[stage0_tile(D_MODEL), stage1_tile(QKV_A_WIDTH), stage1_tile(QKV_B_WIDTH),
                   stage1_tile(GATES_WIDTH)],
        out_shape=[jax.ShapeDtypeStruct((rows, D_MODEL), F32),
                   jax.ShapeDtypeStruct((rows, QKV_A_WIDTH), BF16),
                   jax.ShapeDtypeStruct((rows, QKV_B_WIDTH), BF16),
                   jax.ShapeDtypeStruct((rows, GATES_WIDTH), BF16)],
        scratch_shapes=[pltpu.VMEM((ROW_TILE, D_FF), BF16), pltpu.VMEM((ROW_TILE, D_MODEL), BF16)],
        compiler_params=dense_params,
        name="ffn_in",
    )(x.reshape(rows, D_MODEL), vec(ffn1_norm), ffn1_w_gate.astype(BF16),
      ffn1_w_up.astype(BF16), ffn1_w_down.astype(BF16), vec(mix_norm), w_in_bf16)

    tab_a = (rel_bias.astype(F32) * LOG2_E)[:, _bias_a_table_index()][:, :, None, :]
    n_pairs = A_WIDTH // LANES
    batch_block = lambda width: pl.BlockSpec((None, seq, width), lambda b: (b, 0, 0))
    ya, yb = pl.pallas_call(
        _attn_kernel,
        grid=(batch,),
        in_specs=[pl.BlockSpec(memory_space=pltpu.SMEM),
                  batch_block(QKV_A_WIDTH), batch_block(QKV_B_WIDTH),
                  _resident((A_HEADS, A_KEY_TILES, 1, 2 * A_QBLOCK)),
                  _resident((B_KV_HEADS, B_GROUP * B_QBLOCK, B_WINDOW))],
        out_specs=[batch_block(A_WIDTH), batch_block(B_Q_WIDTH)],
        out_shape=[jax.ShapeDtypeStruct((batch, seq, A_WIDTH), BF16),
                   jax.ShapeDtypeStruct((batch, seq, B_Q_WIDTH), BF16)],
        scratch_shapes=[pltpu.VMEM((n_pairs, A_KEY_TILES, 2 * A_QBLOCK, A_QBLOCK), F32),
                        pltpu.VMEM((seq, 2 * A_WIDTH), BF16),
                        pltpu.VMEM((seq, 2 * LANES), BF16),
                        pltpu.VMEM((2, A_KEY_TILES, 2 * A_QBLOCK, A_QBLOCK), F32)],
        compiler_params=pltpu.CompilerParams(dimension_semantics=("arbitrary",),
                                             vmem_limit_bytes=VMEM_LIMIT_BYTES),
        name="attn",
    )(sinks.astype(F32) * LOG2_E, qkva.reshape(batch, seq, QKV_A_WIDTH),
      qkvb.reshape(batch, seq, QKV_B_WIDTH), tab_a, jnp.asarray(_bias_b_table()))

    out = pl.pallas_call(
        _ffn_out_kernel,
        grid=(n_row_tiles + 1,),
        in_specs=[stage0_tile(D_MODEL), stage0_tile(A_WIDTH), stage0_tile(B_Q_WIDTH),
                  stage0_tile(D_MODEL, col=0), stage0_tile(D_MODEL, col=1),
                  _resident((A_WIDTH, D_MODEL)), _resident((B_Q_WIDTH, D_MODEL)),
                  _resident((D_MODEL, D_MODEL)), _resident((1, D_MODEL)),
                  _resident((D_MODEL, D_FF)), _resident((D_MODEL, D_FF)),
                  _resident((D_FF, D_MODEL)), _resident((1, D_MODEL))],
        out_specs=stage1_tile(D_MODEL),
        out_shape=jax.ShapeDtypeStruct((rows, D_MODEL), F32),
        scratch_shapes=[pltpu.VMEM((ROW_TILE, D_FF), BF16), pltpu.VMEM((ROW_TILE, D_MODEL), F32),
                        pltpu.VMEM((ROW_TILE, D_MODEL), BF16), pltpu.VMEM((ROW_TILE, D_MODEL), BF16)],
        compiler_params=dense_params,
        name="ffn_out",
    )(x1, ya.reshape(rows, A_WIDTH), yb.reshape(rows, B_Q_WIDTH), gates, gates,
      w_proj_a.astype(BF16), w_pb, w_out.astype(BF16), vec(ffn2_norm),
      ffn2_w_gate.astype(BF16), ffn2_w_up.astype(BF16), ffn2_w_down.astype(BF16),
      vec(final_norm))
    return out.reshape(batch, seq, D_MODEL)
```

```python
import functools

import numpy as np
import jax
import jax.numpy as jnp
from jax import lax
from jax.experimental import pallas as pl
from jax.experimental.pallas import tpu as pltpu

F32 = jnp.float32
BF16 = jnp.bfloat16

D_MODEL = 1024
D_FF = 2816
CHUNK = 64
D_HEAD = 64
A_HEADS = 8
A_PREV_CHUNKS = 8
MAX_REL = 128
B_Q_HEADS = 8
B_KV_HEADS = 2
B_GROUP = B_Q_HEADS // B_KV_HEADS
B_PREV_CHUNKS = 2
A_WIDTH = A_HEADS * D_HEAD
B_Q_WIDTH = B_Q_HEADS * D_HEAD
B_KV_WIDTH = B_KV_HEADS * D_HEAD
QKV_A_WIDTH = 3 * A_WIDTH
QKV_B_WIDTH = B_Q_WIDTH + 2 * B_KV_WIDTH
GATES_WIDTH = 2 * D_MODEL
EPS = 1e-6
NEG_INF = -1e30
LOG2_E = float(np.log2(np.e))
QK_SCALE = LOG2_E / float(np.sqrt(D_HEAD))

LANES = 128
MXU_DIM = 256
VMEM_LIMIT_BYTES = 56 * 1024 * 1024

ROW_TILE = 512
FF_CHUNK = MXU_DIM
A_QBLOCK = 4 * CHUNK
A_KEY_TILES = A_PREV_CHUNKS * CHUNK // A_QBLOCK + 1
B_QBLOCK = 2 * CHUNK
B_WINDOW = 2 * B_QBLOCK
assert B_GROUP * B_QBLOCK == 2 * A_QBLOCK and B_WINDOW == A_QBLOCK and B_KV_HEADS <= A_KEY_TILES


def _rms_norm(x, gain):
    return x * lax.rsqrt(jnp.mean(x * x, axis=-1, keepdims=True) + EPS) * gain


def _swiglu(h, wg_ref, wu_ref, wd_ref, hid_ref, between_chunks=None):
    between_chunks = between_chunks or {}
    for c in range(D_FF // FF_CHUNK):
        cols = slice(c * FF_CHUNK, (c + 1) * FF_CHUNK)
        g = jnp.dot(h, wg_ref[:, cols], preferred_element_type=F32)
        u = jnp.dot(h, wu_ref[:, cols], preferred_element_type=F32)
        hid_ref[:, cols] = (g * jax.nn.sigmoid(g) * u).astype(BF16)
        if c in between_chunks:
            between_chunks[c]()
    return jnp.dot(hid_ref[...], wd_ref[...], preferred_element_type=F32)


def _ffn_in_kernel(x_ref, g1_ref, wg_ref, wu_ref, wd_ref, gm_ref, win_ref,
                   x1_ref, qkva_ref, qkvb_ref, gates_ref, hid_ref, h2_ref):
    i = pl.program_id(0)
    last = pl.num_programs(0) - 1

    def ffn_stage():
        x = x_ref[...]
        h = _rms_norm(x, g1_ref[...]).astype(BF16)
        x1 = x + 0.5 * _swiglu(h, wg_ref, wu_ref, wd_ref, hid_ref)
        x1_ref[...] = x1
        h2_ref[...] = _rms_norm(x1, gm_ref[...]).astype(BF16)

    def projection_stage():
        h2 = h2_ref[...]
        col = 0
        for o_ref, q_width in ((qkva_ref, A_WIDTH), (qkvb_ref, B_Q_WIDTH), (gates_ref, 0)):
            for lo, hi, factor in ((0, q_width, QK_SCALE), (q_width, o_ref.shape[1], None)):
                if hi > lo:
                    y = jnp.dot(h2, win_ref[:, col + lo:col + hi], preferred_element_type=F32)
                    o_ref[:, lo:hi] = (y if factor is None else y * factor).astype(BF16)
            col += o_ref.shape[1]

    @pl.when(i == 0)
    def _first():
        ffn_stage()

    @pl.when((i > 0) & (i < last))
    def _steady():
        projection_stage()
        ffn_stage()

    @pl.when(i == last)
    def _last():
        projection_stage()


def _ffn_out_kernel(x1_ref, ya_ref, yb_ref, ga_ref, gb_ref, wpa_ref, wpb_ref, wo_ref,
                    g2_ref, wg_ref, wu_ref, wd_ref, gf_ref, out_ref, hid_ref, x2_ref, h_ref,
                    mg_ref):
    i = pl.program_id(0)
    last = pl.num_programs(0) - 1
    merged_tile = {}

    def gate_columns(c):
        cols = slice(c * MXU_DIM, (c + 1) * MXU_DIM)
        pa = jnp.dot(ya_ref[...], wpa_ref[:, cols], preferred_element_type=F32)
        pb = jnp.dot(yb_ref[...], wpb_ref[:, cols], preferred_element_type=F32)
        mg_ref[:, cols] = (jax.nn.sigmoid(ga_ref[:, cols].astype(F32)) * pa
                           + jax.nn.sigmoid(gb_ref[:, cols].astype(F32)) * pb).astype(BF16)

    def mix_in():
        x2 = x1_ref[...] + jnp.dot(mg_ref[...], wo_ref[...], preferred_element_type=F32)
        merged_tile["x2"] = x2
        merged_tile["h"] = _rms_norm(x2, g2_ref[...]).astype(BF16)

    def store_merged_tile():
        x2_ref[...] = merged_tile["x2"]
        h_ref[...] = merged_tile["h"]

    merge_pieces = [functools.partial(gate_columns, c) for c in range(D_MODEL // MXU_DIM)]
    merge_pieces.append(mix_in)

    def ffn_stage(between_chunks=None):
        x3 = x2_ref[...] + 0.5 * _swiglu(h_ref[...], wg_ref, wu_ref, wd_ref, hid_ref,
                                         between_chunks)
        out_ref[...] = _rms_norm(x3, gf_ref[...])

    @pl.when(i == 0)
    def _first():
        for piece in merge_pieces:
            piece()
        store_merged_tile()

    @pl.when((i > 0) & (i < last))
    def _steady():
        ffn_stage(dict(zip((0, 1, 2, 3, 5), merge_pieces)))
        store_merged_tile()

    @pl.when(i == last)
    def _last():
        ffn_stage()


def _dot_nt(a, b):
    return lax.dot_general(a, b, (((1,), (1,)), ((), ())), preferred_element_type=F32)


def _attn_kernel(sinks_ref, qkva_ref, qkvb_ref, tab_ref, bias_b_ref, ya_ref, yb_ref,
                 bias_a_ref, vext_a_ref, vext_b_ref, kdup_ref, s_ref):
    seq = qkva_ref.shape[0]
    n_pairs = A_WIDTH // LANES
    low = lax.broadcasted_iota(jnp.int32, (1, LANES), 1) < D_HEAD
    ones = jnp.ones((seq, LANES), BF16)

    @pl.when(pl.program_id(0) == 0)
    def _build_bias_a():
        qc = lax.broadcasted_iota(jnp.int32, (A_QBLOCK, A_QBLOCK), 0) // CHUNK
        kc = lax.broadcasted_iota(jnp.int32, (A_QBLOCK, A_QBLOCK), 1) // CHUNK
        for h in range(A_HEADS):
            for d in range(A_KEY_TILES):
                row = jnp.broadcast_to(tab_ref[h, d], (A_QBLOCK, 2 * A_QBLOCK))
                toeplitz = pltpu.roll(row, 0, 1, stride=1, stride_axis=0)[:, :A_QBLOCK]
                chunk_diff = d * (A_QBLOCK // CHUNK) + qc - kc
                visible = (chunk_diff >= 0) & (chunk_diff <= A_PREV_CHUNKS)
                bias_a_ref[h // 2, d, (h % 2) * A_QBLOCK:(h % 2 + 1) * A_QBLOCK, :] = jnp.where(
                    visible, toeplitz, NEG_INF)

    def mixer_a_scores(slot, j, m):
        q = qkva_ref[m * A_QBLOCK:(m + 1) * A_QBLOCK, j * LANES:(j + 1) * LANES]
        zero = jnp.zeros_like(q)
        qs = jnp.concatenate([jnp.where(low, q, zero), jnp.where(low, zero, q)], axis=0)
        kcols = slice(A_WIDTH + j * LANES, A_WIDTH + (j + 1) * LANES)
        dists = [d for d in range(A_KEY_TILES - 1, -1, -1) if m - d >= 0]
        key_rows = [slice((m - d) * A_QBLOCK, (m - d + 1) * A_QBLOCK) for d in dists]
        tile_max = []
        for t, (d, rows) in enumerate(zip(dists, key_rows)):
            logits = _dot_nt(qs, qkva_ref[rows, kcols]) + bias_a_ref[j, d]
            s_ref[slot, t] = logits
            tile_max.append(jnp.max(logits, axis=-1, keepdims=True))
        return functools.reduce(jnp.maximum, tile_max), key_rows

    def mixer_a_finish(slot, j, m, staged):
        mx, key_rows = staged
        acc = None
        for t, rows in enumerate(key_rows):
            pv = jnp.dot(jnp.exp2(s_ref[slot, t] - mx).astype(BF16),
                         vext_a_ref[rows, 2 * j * LANES:2 * (j + 1) * LANES],
                         preferred_element_type=F32)
            acc = pv if acc is None else acc + pv
        o = acc[:, :LANES] / acc[:, LANES:]
        ya_ref[m * A_QBLOCK:(m + 1) * A_QBLOCK, j * LANES:(j + 1) * LANES] = jnp.where(
            low, o[:A_QBLOCK], o[A_QBLOCK:]).astype(BF16)

    def mixer_b_scores(slot, m):
        krows = slice(max(m - 1, 0) * B_QBLOCK, (m + 1) * B_QBLOCK)
        width = krows.stop - krows.start
        maxima = []
        for g in range(B_KV_HEADS):
            tiles = [qkvb_ref[m * B_QBLOCK:(m + 1) * B_QBLOCK, c * LANES:(c + 1) * LANES]
                     for c in range(g * B_GROUP // 2, (g + 1) * B_GROUP // 2)]
            zero = jnp.zeros_like(tiles[0])
            qs = jnp.concatenate([piece for t in tiles
                                  for piece in (jnp.where(low, t, zero), jnp.where(low, zero, t))],
                                 axis=0)
            logits = _dot_nt(qs, kdup_ref[g, krows, :]) + bias_b_ref[g, :, B_WINDOW - width:]
            s_ref[slot, g, :, :width] = logits
            maxima.append(jnp.maximum(jnp.max(logits, axis=-1, keepdims=True), sink[g]))
        return maxima

    def mixer_b_finish(slot, m, staged):
        qrows = slice(m * B_QBLOCK, (m + 1) * B_QBLOCK)
        krows = slice(max(m - 1, 0) * B_QBLOCK, (m + 1) * B_QBLOCK)
        width = krows.stop - krows.start
        for g, mx in enumerate(staged):
            p = jnp.exp2(s_ref[slot, g, :, :width] - jnp.concatenate([mx] * (width // LANES), axis=1))
            pv = jnp.dot(p.astype(BF16), vext_b_ref[g, krows, :], preferred_element_type=F32)
            o = pv[:, :LANES] / (pv[:, LANES:] + jnp.exp2(sink[g] - mx))
            for t in range(B_GROUP // 2):
                c = g * B_GROUP // 2 + t
                even, odd = (o[(2 * t + h) * B_QBLOCK:(2 * t + h + 1) * B_QBLOCK] for h in (0, 1))
                yb_ref[qrows, c * LANES:(c + 1) * LANES] = jnp.where(low, even, odd).astype(BF16)

    for j in range(n_pairs):
        vcols = slice(2 * A_WIDTH + j * LANES, 2 * A_WIDTH + (j + 1) * LANES)
        vext_a_ref[:, 2 * j * LANES:(2 * j + 1) * LANES] = qkva_ref[:, vcols]
        vext_a_ref[:, (2 * j + 1) * LANES:2 * (j + 1) * LANES] = ones
    for g in range(B_KV_HEADS):
        for src, dst in ((qkvb_ref[:, B_Q_WIDTH:B_Q_WIDTH + LANES], kdup_ref.at[g]),
                         (qkvb_ref[:, B_Q_WIDTH + LANES:], vext_b_ref.at[g])):
            half = src[:, g * D_HEAD:(g + 1) * D_HEAD]
            dst[:, :LANES] = jnp.concatenate([half, half], axis=1)
        vext_b_ref[g, :, LANES:] = ones
    sink = [jnp.concatenate([jnp.full((B_QBLOCK, LANES), sinks_ref[g * B_GROUP + c], F32)
                             for c in range(B_GROUP)], axis=0) for g in range(B_KV_HEADS)]

    b_blocks_per_pair = seq // B_QBLOCK // n_pairs
    blocks = []
    for j in range(n_pairs):
        blocks += [(functools.partial(mixer_a_scores, j=j, m=m),
                    functools.partial(mixer_a_finish, j=j, m=m)) for m in range(seq // A_QBLOCK)]
        blocks += [(functools.partial(mixer_b_scores, m=m), functools.partial(mixer_b_finish, m=m))
                   for m in range(j * b_blocks_per_pair, (j + 1) * b_blocks_per_pair)]
    staged = blocks[0][0](slot=0)
    for n, (_, finish) in enumerate(blocks):
        next_staged = blocks[n + 1][0](slot=(n + 1) % 2) if n + 1 < len(blocks) else None
        finish(slot=n % 2, staged=staged)
        staged = next_staged


def _resident(shape):
    return pl.BlockSpec(shape, lambda *_: (0,) * len(shape), pipeline_mode=pl.Buffered(1))


def _bias_a_table_index():
    u = np.arange(2 * A_QBLOCK)
    key_minus_query = np.where(u <= A_QBLOCK, u, u - 2 * A_QBLOCK)
    rel = np.arange(A_KEY_TILES)[:, None] * A_QBLOCK - key_minus_query[None, :]
    return np.clip(rel, -(CHUNK - 1), MAX_REL) + (CHUNK - 1)


def _bias_b_table():
    slopes = np.array([2.0 ** (-8.0 * (h + 1) / B_Q_HEADS) for h in range(B_Q_HEADS)], np.float32)
    i = np.arange(B_QBLOCK)[:, None]
    j = np.arange(B_WINDOW)[None, :]
    rel = i - j + B_QBLOCK
    chunk_diff = i // CHUNK - j // CHUNK + B_QBLOCK // CHUNK
    vis = (chunk_diff >= 0) & (chunk_diff <= B_PREV_CHUNKS)
    dist = np.abs(rel).astype(np.float32)
    out = np.empty((B_KV_HEADS, B_GROUP * B_QBLOCK, B_WINDOW), np.float32)
    for g in range(B_KV_HEADS):
        for c in range(B_GROUP):
            out[g, c * B_QBLOCK:(c + 1) * B_QBLOCK] = np.where(
                vis, -slopes[g * B_GROUP + c] * dist * LOG2_E, NEG_INF)
    return out


def kernel(x, ffn1_norm, ffn1_w_gate, ffn1_w_up, ffn1_w_down, mix_norm, w_in, rel_bias, sinks,
           w_proj_a, w_proj_b, w_out, ffn2_norm, ffn2_w_gate, ffn2_w_up, ffn2_w_down, final_norm):
    batch, seq, d = x.shape
    rows = batch * seq
    assert d == D_MODEL and rows % ROW_TILE == 0 and seq % A_QBLOCK == 0
    n_row_tiles = rows // ROW_TILE
    vec = lambda g: g.reshape(1, D_MODEL).astype(F32)

    dense_params = pltpu.CompilerParams(dimension_semantics=("arbitrary",),
                                        vmem_limit_bytes=VMEM_LIMIT_BYTES)

    stage0_tile = lambda width, col=0: pl.BlockSpec(
        (ROW_TILE, width), lambda i: (jnp.minimum(i, n_row_tiles - 1), col))
    stage1_tile = lambda width: pl.BlockSpec(
        (ROW_TILE, width), lambda i: (jnp.maximum(i - 1, 0), 0))
    x1, qkva, qkvb, gates = pl.pallas_call(
        _ffn_in_kernel,
        grid=(n_row_tiles + 1,),
        in_specs=[stage0_tile(D_MODEL), _resident((1, D_MODEL)),
                  _resident((D_MODEL, D_FF)), _resident((D_MODEL, D_FF)),
                  _resident((D_FF, D_MODEL)), _resident((1, D_MODEL)),
                  _resident(w_in.shape)],
        out_specs=[stage0_tile(D_MODEL), stage1_tile(QKV_A_WIDTH), stage1_tile(QKV_B_WIDTH),
                   stage1_tile(GATES_WIDTH)],
        out_shape=[jax.ShapeDtypeStruct((rows, D_MODEL), F32),
                   jax.ShapeDtypeStruct((rows, QKV_A_WIDTH), BF16),
                   jax.ShapeDtypeStruct((rows, QKV_B_WIDTH), BF16),
                   jax.ShapeDtypeStruct((rows, GATES_WIDTH), BF16)],
        scratch_shapes=[pltpu.VMEM((ROW_TILE, D_FF), BF16), pltpu.VMEM((ROW_TILE, D_MODEL), BF16)],
        compiler_params=dense_params,
        name="ffn_in",
    )(x.reshape(rows, D_MODEL), vec(ffn1_norm), ffn1_w_gate.astype(BF16),
      ffn1_w_up.astype(BF16), ffn1_w_down.astype(BF16), vec(mix_norm), w_in.astype(BF16))

    tab_a = (rel_bias.astype(F32) * LOG2_E)[:, _bias_a_table_index()][:, :, None, :]
    n_pairs = A_WIDTH // LANES
    batch_block = lambda width: pl.BlockSpec((None, seq, width), lambda b: (b, 0, 0))
    ya, yb = pl.pallas_call(
        _attn_kernel,
        grid=(batch,),
        in_specs=[pl.BlockSpec(memory_space=pltpu.SMEM),
                  batch_block(QKV_A_WIDTH), batch_block(QKV_B_WIDTH),
                  _resident((A_HEADS, A_KEY_TILES, 1, 2 * A_QBLOCK)),
                  _resident((B_KV_HEADS, B_GROUP * B_QBLOCK, B_WINDOW))],
        out_specs=[batch_block(A_WIDTH), batch_block(B_Q_WIDTH)],
        out_shape=[jax.ShapeDtypeStruct((batch, seq, A_WIDTH), BF16),
                   jax.ShapeDtypeStruct((batch, seq, B_Q_WIDTH), BF16)],
        scratch_shapes=[pltpu.VMEM((n_pairs, A_KEY_TILES, 2 * A_QBLOCK, A_QBLOCK), F32),
                        pltpu.VMEM((seq, 2 * A_WIDTH), BF16),
                        pltpu.VMEM((B_KV_HEADS, seq, 2 * LANES), BF16),
                        pltpu.VMEM((B_KV_HEADS, seq, LANES), BF16),
                        pltpu.VMEM((2, A_KEY_TILES, 2 * A_QBLOCK, A_QBLOCK), F32)],
        compiler_params=pltpu.CompilerParams(dimension_semantics=("arbitrary",),
                                             vmem_limit_bytes=VMEM_LIMIT_BYTES),
        name="attn",
    )(sinks.astype(F32) * LOG2_E, qkva.reshape(batch, seq, QKV_A_WIDTH),
      qkvb.reshape(batch, seq, QKV_B_WIDTH), tab_a, jnp.asarray(_bias_b_table()))

    out = pl.pallas_call(
        _ffn_out_kernel,
        grid=(n_row_tiles + 1,),
        in_specs=[stage0_tile(D_MODEL), stage0_tile(A_WIDTH), stage0_tile(B_Q_WIDTH),
                  stage0_tile(D_MODEL, col=0), stage0_tile(D_MODEL, col=1),
                  _resident((A_WIDTH, D_MODEL)), _resident((B_Q_WIDTH, D_MODEL)),
                  _resident((D_MODEL, D_MODEL)), _resident((1, D_MODEL)),
                  _resident((D_MODEL, D_FF)), _resident((D_MODEL, D_FF)),
                  _resident((D_FF, D_MODEL)), _resident((1, D_MODEL))],
        out_specs=stage1_tile(D_MODEL),
        out_shape=jax.ShapeDtypeStruct((rows, D_MODEL), F32),
        scratch_shapes=[pltpu.VMEM((ROW_TILE, D_FF), BF16), pltpu.VMEM((ROW_TILE, D_MODEL), F32),
                        pltpu.VMEM((ROW_TILE, D_MODEL), BF16), pltpu.VMEM((ROW_TILE, D_MODEL), BF16)],
        compiler_params=dense_params,
        name="ffn_out",
    )(x1, ya.reshape(rows, A_WIDTH), yb.reshape(rows, B_Q_WIDTH), gates, gates,
      w_proj_a.astype(BF16), w_proj_b.astype(BF16), w_out.astype(BF16), vec(ffn2_norm),
      ffn2_w_gate.astype(BF16), ffn2_w_up.astype(BF16), ffn2_w_down.astype(BF16),
      vec(final_norm))
    return out.reshape(batch, seq, D_MODEL)
```

```python
import functools

import numpy as np
import jax
import jax.numpy as jnp
from jax import lax
from jax.experimental import pallas as pl
from jax.experimental.pallas import tpu as pltpu

F32 = jnp.float32
BF16 = jnp.bfloat16

D_MODEL = 1024
D_FF = 2816
CHUNK = 64
D_HEAD = 64
A_HEADS = 8
A_PREV_CHUNKS = 8
MAX_REL = 128
B_Q_HEADS = 8
B_KV_HEADS = 2
B_GROUP = B_Q_HEADS // B_KV_HEADS
B_PREV_CHUNKS = 2
A_WIDTH = A_HEADS * D_HEAD
B_Q_WIDTH = B_Q_HEADS * D_HEAD
B_KV_WIDTH = B_KV_HEADS * D_HEAD
QKV_A_WIDTH = 3 * A_WIDTH
QKV_B_WIDTH = B_Q_WIDTH + 2 * B_KV_WIDTH
GATES_WIDTH = 2 * D_MODEL
EPS = 1e-6
NEG_INF = -1e30
LOG2_E = float(np.log2(np.e))
QK_SCALE = LOG2_E / float(np.sqrt(D_HEAD))

LANES = 128
BF16_SUBLANES = 16
MXU_DIM = 256
VMEM_LIMIT_BYTES = 56 * 1024 * 1024

ROW_TILE = 512
FF_CHUNK = MXU_DIM
A_QBLOCK = 4 * CHUNK
A_KEY_TILES = A_PREV_CHUNKS * CHUNK // A_QBLOCK + 1
B_QBLOCK = 2 * CHUNK
B_WINDOW = 2 * B_QBLOCK
assert B_GROUP * B_QBLOCK == 2 * A_QBLOCK and B_WINDOW == A_QBLOCK and B_KV_HEADS <= A_KEY_TILES


def _rms_norm(x, gain):
    return x * lax.rsqrt(jnp.mean(x * x, axis=-1, keepdims=True) + EPS) * gain


def _swiglu(h, wg_ref, wu_ref, wd_ref, hid_ref, between_chunks=None):
    between_chunks = between_chunks or {}
    for c in range(D_FF // FF_CHUNK):
        cols = slice(c * FF_CHUNK, (c + 1) * FF_CHUNK)
        g = jnp.dot(h, wg_ref[:, cols], preferred_element_type=F32)
        u = jnp.dot(h, wu_ref[:, cols], preferred_element_type=F32)
        hid_ref[:, cols] = (g * jax.nn.sigmoid(g) * u).astype(BF16)
        if c in between_chunks:
            between_chunks[c]()
    return jnp.dot(hid_ref[...], wd_ref[...], preferred_element_type=F32)


def _ffn_in_kernel(x_ref, g1_ref, wg_ref, wu_ref, wd_ref, gm_ref, win_ref,
                   wg2_ref, wu2_ref, wd2_ref,
                   x1_ref, qkva_ref, qkvb_ref, gates_ref, wg2_bf16_ref, wu2_bf16_ref, wd2_bf16_ref,
                   hid_ref, h2_ref):
    i = pl.program_id(0)
    last = pl.num_programs(0) - 1

    def cast_ffn2_weight_blocks():
        for src, dst in ((wg2_ref, wg2_bf16_ref), (wu2_ref, wu2_bf16_ref),
                         (wd2_ref, wd2_bf16_ref)):
            dst[...] = src[...].astype(BF16)

    def ffn_stage():
        x = x_ref[...]
        h = _rms_norm(x, g1_ref[...]).astype(BF16)
        x1 = x + 0.5 * _swiglu(h, wg_ref, wu_ref, wd_ref, hid_ref)
        x1_ref[...] = x1
        h2_ref[...] = _rms_norm(x1, gm_ref[...]).astype(BF16)

    def projection_stage():
        h2 = h2_ref[...]
        col = 0
        for o_ref, q_width in ((qkva_ref, A_WIDTH), (qkvb_ref, B_Q_WIDTH), (gates_ref, 0)):
            for lo, hi, factor in ((0, q_width, QK_SCALE), (q_width, o_ref.shape[1], None)):
                if hi > lo:
                    y = jnp.dot(h2, win_ref[:, col + lo:col + hi], preferred_element_type=F32)
                    o_ref[:, lo:hi] = (y if factor is None else y * factor).astype(BF16)
            col += o_ref.shape[1]

    @pl.when(i == 0)
    def _first():
        cast_ffn2_weight_blocks()
        ffn_stage()

    @pl.when((i > 0) & (i < last))
    def _steady():
        cast_ffn2_weight_blocks()
        projection_stage()
        ffn_stage()

    @pl.when(i == last)
    def _last():
        cast_ffn2_weight_blocks()
        projection_stage()


def _ffn_out_kernel(x1_ref, ya_ref, yb_ref, ga_ref, gb_ref, wpa_ref, wpb_ref, wo_ref,
                    g2_ref, wg_ref, wu_ref, wd_ref, gf_ref, out_ref, hid_ref, x2_ref, h_ref,
                    mg_ref):
    i = pl.program_id(0)
    last = pl.num_programs(0) - 1
    merged_tile = {}

    def gate_columns(c):
        cols = slice(c * MXU_DIM, (c + 1) * MXU_DIM)
        pa = jnp.dot(ya_ref[...], wpa_ref[:, cols], preferred_element_type=F32)
        pb = jnp.dot(yb_ref[...], wpb_ref[:, cols], preferred_element_type=F32)
        mg_ref[:, cols] = (jax.nn.sigmoid(ga_ref[:, cols].astype(F32)) * pa
                           + jax.nn.sigmoid(gb_ref[:, cols].astype(F32)) * pb).astype(BF16)

    def mix_in():
        x2 = x1_ref[...] + jnp.dot(mg_ref[...], wo_ref[...], preferred_element_type=F32)
        merged_tile["x2"] = x2
        merged_tile["h"] = _rms_norm(x2, g2_ref[...]).astype(BF16)

    def store_merged_tile():
        x2_ref[...] = merged_tile["x2"]
        h_ref[...] = merged_tile["h"]

    merge_pieces = [functools.partial(gate_columns, c) for c in range(D_MODEL // MXU_DIM)]
    merge_pieces.append(mix_in)

    def ffn_stage(between_chunks=None):
        x3 = x2_ref[...] + 0.5 * _swiglu(h_ref[...], wg_ref, wu_ref, wd_ref, hid_ref,
                                         between_chunks)
        out_ref[...] = _rms_norm(x3, gf_ref[...])

    @pl.when(i == 0)
    def _first():
        for piece in merge_pieces:
            piece()
        store_merged_tile()

    @pl.when((i > 0) & (i < last))
    def _steady():
        ffn_stage(dict(zip((0, 1, 2, 3, 5), merge_pieces)))
        store_merged_tile()

    @pl.when(i == last)
    def _last():
        ffn_stage()


def _dot_nt(a, b):
    return lax.dot_general(a, b, (((1,), (1,)), ((), ())), preferred_element_type=F32)


def _attn_kernel(sinks_ref, qkva_ref, qkvb_ref, tab_ref, bias_b_ref, ya_ref, yb_ref,
                 bias_a_ref, vext_a_ref, vext_b_ref, kdup_ref, s_ref):
    seq = qkva_ref.shape[0]
    n_pairs = A_WIDTH // LANES
    low = lax.broadcasted_iota(jnp.int32, (1, LANES), 1) < D_HEAD
    ones = jnp.ones((seq, LANES), BF16)

    @pl.when(pl.program_id(0) == 0)
    def _build_bias_a():
        qc = lax.broadcasted_iota(jnp.int32, (A_QBLOCK, A_QBLOCK), 0) // CHUNK
        kc = lax.broadcasted_iota(jnp.int32, (A_QBLOCK, A_QBLOCK), 1) // CHUNK
        for h in range(A_HEADS):
            for d in range(A_KEY_TILES):
                row = jnp.broadcast_to(tab_ref[h, d], (A_QBLOCK, 2 * A_QBLOCK))
                toeplitz = pltpu.roll(row, 0, 1, stride=1, stride_axis=0)[:, :A_QBLOCK]
                chunk_diff = d * (A_QBLOCK // CHUNK) + qc - kc
                visible = (chunk_diff >= 0) & (chunk_diff <= A_PREV_CHUNKS)
                bias_a_ref[h // 2, d, (h % 2) * A_QBLOCK:(h % 2 + 1) * A_QBLOCK, :] = jnp.where(
                    visible, toeplitz, NEG_INF)

    def mixer_a_scores(slot, j, m):
        q = qkva_ref[m * A_QBLOCK:(m + 1) * A_QBLOCK, j * LANES:(j + 1) * LANES]
        zero = jnp.zeros_like(q)
        qs = jnp.concatenate([jnp.where(low, q, zero), jnp.where(low, zero, q)], axis=0)
        kcols = slice(A_WIDTH + j * LANES, A_WIDTH + (j + 1) * LANES)
        dists = [d for d in range(A_KEY_TILES - 1, -1, -1) if m - d >= 0]
        key_rows = [slice((m - d) * A_QBLOCK, (m - d + 1) * A_QBLOCK) for d in dists]
        tile_max = []
        for t, (d, rows) in enumerate(zip(dists, key_rows)):
            logits = _dot_nt(qs, qkva_ref[rows, kcols]) + bias_a_ref[j, d]
            s_ref[slot, t] = logits
            tile_max.append(jnp.max(logits, axis=-1, keepdims=True))
        return functools.reduce(jnp.maximum, tile_max), key_rows

    def mixer_a_finish(slot, j, m, staged):
        mx, key_rows = staged
        acc = None
        for t, rows in enumerate(key_rows):
            pv = jnp.dot(jnp.exp2(s_ref[slot, t] - mx).astype(BF16),
                         vext_a_ref[rows, 2 * j * LANES:2 * (j + 1) * LANES],
                         preferred_element_type=F32)
            acc = pv if acc is None else acc + pv
        o = acc[:, :LANES] / acc[:, LANES:]
        ya_ref[m * A_QBLOCK:(m + 1) * A_QBLOCK, j * LANES:(j + 1) * LANES] = jnp.where(
            low, o[:A_QBLOCK], o[A_QBLOCK:]).astype(BF16)

    def mixer_b_scores(slot, m):
        krows = slice(max(m - 1, 0) * B_QBLOCK, (m + 1) * B_QBLOCK)
        width = krows.stop - krows.start
        maxima = []
        for g in range(B_KV_HEADS):
            tiles = [qkvb_ref[m * B_QBLOCK:(m + 1) * B_QBLOCK, c * LANES:(c + 1) * LANES]
                     for c in range(g * B_GROUP // 2, (g + 1) * B_GROUP // 2)]
            zero = jnp.zeros_like(tiles[0])
            qs = jnp.concatenate([piece for t in tiles
                                  for piece in (jnp.where(low, t, zero), jnp.where(low, zero, t))],
                                 axis=0)
            logits = _dot_nt(qs, kdup_ref[g, krows, :]) + bias_b_ref[g, :, B_WINDOW - width:]
            s_ref[slot, g, :, :width] = logits
            maxima.append(jnp.maximum(jnp.max(logits, axis=-1, keepdims=True), sink[g]))
        return maxima

    def mixer_b_finish(slot, m, staged):
        qrows = slice(m * B_QBLOCK, (m + 1) * B_QBLOCK)
        krows = slice(max(m - 1, 0) * B_QBLOCK, (m + 1) * B_QBLOCK)
        width = krows.stop - krows.start
        for g, mx in enumerate(staged):
            p = jnp.exp2(s_ref[slot, g, :, :width] - jnp.concatenate([mx] * (width // LANES), axis=1))
            pv = jnp.dot(p.astype(BF16), vext_b_ref[g, krows, :], preferred_element_type=F32)
            o = pv[:, :LANES] / (pv[:, LANES:] + jnp.exp2(sink[g] - mx))
            for t in range(B_GROUP // 2):
                c = g * B_GROUP // 2 + t
                even, odd = (o[(2 * t + h) * B_QBLOCK:(2 * t + h + 1) * B_QBLOCK] for h in (0, 1))
                yb_ref[qrows, c * LANES:(c + 1) * LANES] = jnp.where(low, even, odd).astype(BF16)

    for j in range(n_pairs):
        vcols = slice(2 * A_WIDTH + j * LANES, 2 * A_WIDTH + (j + 1) * LANES)
        vext_a_ref[:, 2 * j * LANES:(2 * j + 1) * LANES] = qkva_ref[:, vcols]
        vext_a_ref[:, (2 * j + 1) * LANES:2 * (j + 1) * LANES] = ones
    for g in range(B_KV_HEADS):
        for src, dst in ((qkvb_ref[:, B_Q_WIDTH:B_Q_WIDTH + LANES], kdup_ref.at[g]),
                         (qkvb_ref[:, B_Q_WIDTH + LANES:], vext_b_ref.at[g])):
            half = src[:, g * D_HEAD:(g + 1) * D_HEAD]
            dst[:, :LANES] = jnp.concatenate([half, half], axis=1)
        vext_b_ref[g, :, LANES:] = ones
    sink = [jnp.concatenate([jnp.full((B_QBLOCK, LANES), sinks_ref[g * B_GROUP + c], F32)
                             for c in range(B_GROUP)], axis=0) for g in range(B_KV_HEADS)]

    b_blocks_per_pair = seq // B_QBLOCK // n_pairs
    blocks = []
    for j in range(n_pairs):
        blocks += [(functools.partial(mixer_a_scores, j=j, m=m),
                    functools.partial(mixer_a_finish, j=j, m=m)) for m in range(seq // A_QBLOCK)]
        blocks += [(functools.partial(mixer_b_scores, m=m), functools.partial(mixer_b_finish, m=m))
                   for m in range(j * b_blocks_per_pair, (j + 1) * b_blocks_per_pair)]
    staged = blocks[0][0](slot=0)
    for n, (_, finish) in enumerate(blocks):
        next_staged = blocks[n + 1][0](slot=(n + 1) % 2) if n + 1 < len(blocks) else None
        finish(slot=n % 2, staged=staged)
        staged = next_staged


def _resident(shape):
    return pl.BlockSpec(shape, lambda *_: (0,) * len(shape), pipeline_mode=pl.Buffered(1))


def _bias_a_table_index():
    u = np.arange(2 * A_QBLOCK)
    key_minus_query = np.where(u <= A_QBLOCK, u, u - 2 * A_QBLOCK)
    rel = np.arange(A_KEY_TILES)[:, None] * A_QBLOCK - key_minus_query[None, :]
    return np.clip(rel, -(CHUNK - 1), MAX_REL) + (CHUNK - 1)


def _bias_b_table():
    slopes = np.array([2.0 ** (-8.0 * (h + 1) / B_Q_HEADS) for h in range(B_Q_HEADS)], np.float32)
    i = np.arange(B_QBLOCK)[:, None]
    j = np.arange(B_WINDOW)[None, :]
    rel = i - j + B_QBLOCK
    chunk_diff = i // CHUNK - j // CHUNK + B_QBLOCK // CHUNK
    vis = (chunk_diff >= 0) & (chunk_diff <= B_PREV_CHUNKS)
    dist = np.abs(rel).astype(np.float32)
    out = np.empty((B_KV_HEADS, B_GROUP * B_QBLOCK, B_WINDOW), np.float32)
    for g in range(B_KV_HEADS):
        for c in range(B_GROUP):
            out[g, c * B_QBLOCK:(c + 1) * B_QBLOCK] = np.where(
                vis, -slopes[g * B_GROUP + c] * dist * LOG2_E, NEG_INF)
    return out


def kernel(x, ffn1_norm, ffn1_w_gate, ffn1_w_up, ffn1_w_down, mix_norm, w_in, rel_bias, sinks,
           w_proj_a, w_proj_b, w_out, ffn2_norm, ffn2_w_gate, ffn2_w_up, ffn2_w_down, final_norm):
    batch, seq, d = x.shape
    rows = batch * seq
    assert d == D_MODEL and rows % ROW_TILE == 0 and seq % A_QBLOCK == 0
    n_row_tiles = rows // ROW_TILE
    vec = lambda g: g.reshape(1, D_MODEL).astype(F32)

    dense_params = pltpu.CompilerParams(dimension_semantics=("arbitrary",),
                                        vmem_limit_bytes=VMEM_LIMIT_BYTES)

    stage0_tile = lambda width, col=0: pl.BlockSpec(
        (ROW_TILE, width), lambda i: (jnp.minimum(i, n_row_tiles - 1), col))
    stage1_tile = lambda width: pl.BlockSpec(
        (ROW_TILE, width), lambda i: (jnp.maximum(i - 1, 0), 0))
    def cast_block(weight):
        n_rows, width = weight.shape
        block_rows = next(r for r in range(BF16_SUBLANES, n_rows + 1, BF16_SUBLANES)
                          if n_rows % r == 0 and n_rows // r <= n_row_tiles)
        return pl.BlockSpec((block_rows, width),
                            lambda i: (jnp.minimum(i, n_rows // block_rows - 1), 0))
    ffn2_weights = (ffn2_w_gate, ffn2_w_up, ffn2_w_down)
    x1, qkva, qkvb, gates, wg2, wu2, wd2 = pl.pallas_call(
        _ffn_in_kernel,
        grid=(n_row_tiles + 1,),
        in_specs=[stage0_tile(D_MODEL), _resident((1, D_MODEL)),
                  _resident((D_MODEL, D_FF)), _resident((D_MODEL, D_FF)),
                  _resident((D_FF, D_MODEL)), _resident((1, D_MODEL)),
                  _resident(w_in.shape)] + [cast_block(w) for w in ffn2_weights],
        out_specs=[stage0_tile(D_MODEL), stage1_tile(QKV_A_WIDTH), stage1_tile(QKV_B_WIDTH),
                   stage1_tile(GATES_WIDTH)] + [cast_block(w) for w in ffn2_weights],
        out_shape=[jax.ShapeDtypeStruct((rows, D_MODEL), F32),
                   jax.ShapeDtypeStruct((rows, QKV_A_WIDTH), BF16),
                   jax.ShapeDtypeStruct((rows, QKV_B_WIDTH), BF16),
                   jax.ShapeDtypeStruct((rows, GATES_WIDTH), BF16)]
                  + [jax.ShapeDtypeStruct(w.shape, BF16) for w in ffn2_weights],
        scratch_shapes=[pltpu.VMEM((ROW_TILE, D_FF), BF16), pltpu.VMEM((ROW_TILE, D_MODEL), BF16)],
        compiler_params=dense_params,
        name="ffn_in",
    )(x.reshape(rows, D_MODEL), vec(ffn1_norm), ffn1_w_gate.astype(BF16),
      ffn1_w_up.astype(BF16), ffn1_w_down.astype(BF16), vec(mix_norm), w_in.astype(BF16), *ffn2_weights)

    tab_a = (rel_bias.astype(F32) * LOG2_E)[:, _bias_a_table_index()][:, :, None, :]
    n_pairs = A_WIDTH // LANES
    batch_block = lambda width: pl.BlockSpec((None, seq, width), lambda b: (b, 0, 0))
    ya, yb = pl.pallas_call(
        _attn_kernel,
        grid=(batch,),
        in_specs=[pl.BlockSpec(memory_space=pltpu.SMEM),
                  batch_block(QKV_A_WIDTH), batch_block(QKV_B_WIDTH),
                  _resident((A_HEADS, A_KEY_TILES, 1, 2 * A_QBLOCK)),
                  _resident((B_KV_HEADS, B_GROUP * B_QBLOCK, B_WINDOW))],
        out_specs=[batch_block(A_WIDTH), batch_block(B_Q_WIDTH)],
        out_shape=[jax.ShapeDtypeStruct((batch, seq, A_WIDTH), BF16),
                   jax.ShapeDtypeStruct((batch, seq, B_Q_WIDTH), BF16)],
        scratch_shapes=[pltpu.VMEM((n_pairs, A_KEY_TILES, 2 * A_QBLOCK, A_QBLOCK), F32),
                        pltpu.VMEM((seq, 2 * A_WIDTH), BF16),
                        pltpu.VMEM((B_KV_HEADS, seq, 2 * LANES), BF16),
                        pltpu.VMEM((B_KV_HEADS, seq, LANES), BF16),
                        pltpu.VMEM((2, A_KEY_TILES, 2 * A_QBLOCK, A_QBLOCK), F32)],
        compiler_params=pltpu.CompilerParams(dimension_semantics=("arbitrary",),
                                             vmem_limit_bytes=VMEM_LIMIT_BYTES),
        name="attn",
    )(sinks.astype(F32) * LOG2_E, qkva.reshape(batch, seq, QKV_A_WIDTH),
      qkvb.reshape(batch, seq, QKV_B_WIDTH), tab_a, jnp.asarray(_bias_b_table()))

    out = pl.pallas_call(
        _ffn_out_kernel,
        grid=(n_row_tiles + 1,),
        in_specs=[stage0_tile(D_MODEL), stage0_tile(A_WIDTH), stage0_tile(B_Q_WIDTH),
                  stage0_tile(D_MODEL, col=0), stage0_tile(D_MODEL, col=1),
                  _resident((A_WIDTH, D_MODEL)), _resident((B_Q_WIDTH, D_MODEL)),
                  _resident((D_MODEL, D_MODEL)), _resident((1, D_MODEL)),
                  _resident((D_MODEL, D_FF)), _resident((D_MODEL, D_FF)),
                  _resident((D_FF, D_MODEL)), _resident((1, D_MODEL))],
        out_specs=stage1_tile(D_MODEL),
        out_shape=jax.ShapeDtypeStruct((rows, D_MODEL), F32),
        scratch_shapes=[pltpu.VMEM((ROW_TILE, D_FF), BF16), pltpu.VMEM((ROW_TILE, D_MODEL), F32),
                        pltpu.VMEM((ROW_TILE, D_MODEL), BF16), pltpu.VMEM((ROW_TILE, D_MODEL), BF16)],
        compiler_params=dense_params,
        name="ffn_out",
    )(x1, ya.reshape(rows, A_WIDTH), yb.reshape(rows, B_Q_WIDTH), gates, gates,
      w_proj_a.astype(BF16), w_proj_b.astype(BF16), w_out.astype(BF16), vec(ffn2_norm),
      wg2, wu2, wd2, vec(final_norm))
    return out.reshape(batch, seq, D_MODEL)
```

```python
import functools

import numpy as np
import jax
import jax.numpy as jnp
from jax import lax
from jax.experimental import pallas as pl
from jax.experimental.pallas import tpu as pltpu

F32 = jnp.float32
BF16 = jnp.bfloat16

D_MODEL = 1024
D_FF = 2816
CHUNK = 64
D_HEAD = 64
A_HEADS = 8
A_PREV_CHUNKS = 8
MAX_REL = 128
B_Q_HEADS = 8
B_KV_HEADS = 2
B_GROUP = B_Q_HEADS // B_KV_HEADS
B_PREV_CHUNKS = 2
A_WIDTH = A_HEADS * D_HEAD
B_Q_WIDTH = B_Q_HEADS * D_HEAD
B_KV_WIDTH = B_KV_HEADS * D_HEAD
QKV_A_WIDTH = 3 * A_WIDTH
QKV_B_WIDTH = B_Q_WIDTH + 2 * B_KV_WIDTH
GATES_WIDTH = 2 * D_MODEL
EPS = 1e-6
NEG_INF = -1e30
LOG2_E = float(np.log2(np.e))
QK_SCALE = LOG2_E / float(np.sqrt(D_HEAD))

LANES = 128
BF16_SUBLANES = 16
MXU_DIM = 256
VMEM_LIMIT_BYTES = 56 * 1024 * 1024

ROW_TILE = 512
FF_CHUNK = MXU_DIM
A_QBLOCK = 4 * CHUNK
A_KEY_TILES = A_PREV_CHUNKS * CHUNK // A_QBLOCK + 1
B_QBLOCK = 2 * CHUNK
B_WINDOW = 2 * B_QBLOCK
assert B_GROUP * B_QBLOCK == 2 * A_QBLOCK and B_WINDOW == A_QBLOCK and B_KV_HEADS <= A_KEY_TILES


def _rms_norm(x, gain):
    return x * lax.rsqrt(jnp.mean(x * x, axis=-1, keepdims=True) + EPS) * gain


def _swiglu(h, wg_ref, wu_ref, wd_ref, hid_ref, between_chunks=None):
    between_chunks = between_chunks or {}
    for c in range(D_FF // FF_CHUNK):
        cols = slice(c * FF_CHUNK, (c + 1) * FF_CHUNK)
        g = jnp.dot(h, wg_ref[:, cols], preferred_element_type=F32)
        u = jnp.dot(h, wu_ref[:, cols], preferred_element_type=F32)
        hid_ref[:, cols] = (g * jax.nn.sigmoid(g) * u).astype(BF16)
        if c in between_chunks:
            between_chunks[c]()
    return jnp.dot(hid_ref[...], wd_ref[...], preferred_element_type=F32)


def _ffn_in_kernel(x_ref, g1_ref, wg_ref, wu_ref, wd_ref, gm_ref, win_ref,
                   x1_ref, qkva_ref, qkvb_ref, gates_ref, hid_ref, h2_ref):
    i = pl.program_id(0)
    last = pl.num_programs(0) - 1

    def ffn_stage(trailing_pieces=()):
        x = x_ref[...]
        h = _rms_norm(x, g1_ref[...]).astype(BF16)
        x1 = x + 0.5 * _swiglu(h, wg_ref, wu_ref, wd_ref, hid_ref)
        x1_ref[...] = x1
        h2 = _rms_norm(x1, gm_ref[...]).astype(BF16)
        for piece in trailing_pieces:
            piece()
        h2_ref[...] = h2

    def project(o_ref, lo, hi, col, factor):
        y = jnp.dot(h2_ref[...], win_ref[:, col + lo:col + hi], preferred_element_type=F32)
        o_ref[:, lo:hi] = (y if factor is None else y * factor).astype(BF16)

    projection_pieces, col = [], 0
    for o_ref, q_width in ((qkva_ref, A_WIDTH), (qkvb_ref, B_Q_WIDTH), (gates_ref, 0)):
        for lo in range(0, o_ref.shape[1], 2 * MXU_DIM):
            hi = min(lo + 2 * MXU_DIM, o_ref.shape[1])
            factor = QK_SCALE if hi <= q_width else None
            projection_pieces.append(functools.partial(project, o_ref, lo, hi, col, factor))
        col += o_ref.shape[1]
    n_trailing = 3

    @pl.when(i == 0)
    def _first():
        ffn_stage()

    @pl.when((i > 0) & (i < last))
    def _steady():
        for piece in projection_pieces[:-n_trailing]:
            piece()
        ffn_stage(projection_pieces[-n_trailing:])

    @pl.when(i == last)
    def _last():
        for piece in projection_pieces:
            piece()


def _ffn_out_kernel(x1_ref, ya_ref, yb_ref, ga_ref, gb_ref, wpa_ref, wpb_ref, wo_ref,
                    g2_ref, wg_ref, wu_ref, wd_ref, gf_ref, out_ref, hid_ref, x2_ref, h_ref,
                    mg_ref):
    i = pl.program_id(0)
    last = pl.num_programs(0) - 1
    merged_tile = {}

    def gate_columns(c):
        cols = slice(c * MXU_DIM, (c + 1) * MXU_DIM)
        pa = jnp.dot(ya_ref[...], wpa_ref[:, cols], preferred_element_type=F32)
        pb = jnp.dot(yb_ref[...], wpb_ref[:, cols], preferred_element_type=F32)
        mg_ref[:, cols] = (jax.nn.sigmoid(ga_ref[:, cols].astype(F32)) * pa
                           + jax.nn.sigmoid(gb_ref[:, cols].astype(F32)) * pb).astype(BF16)

    def mix_in():
        x2 = x1_ref[...] + jnp.dot(mg_ref[...], wo_ref[...], preferred_element_type=F32)
        merged_tile["x2"] = x2
        merged_tile["h"] = _rms_norm(x2, g2_ref[...]).astype(BF16)

    def store_merged_tile():
        x2_ref[...] = merged_tile["x2"]
        h_ref[...] = merged_tile["h"]

    merge_pieces = [functools.partial(gate_columns, c) for c in range(D_MODEL // MXU_DIM)]
    merge_pieces.append(mix_in)

    def ffn_stage(between_chunks=None):
        x3 = x2_ref[...] + 0.5 * _swiglu(h_ref[...], wg_ref, wu_ref, wd_ref, hid_ref,
                                         between_chunks)
        out_ref[...] = _rms_norm(x3, gf_ref[...])

    @pl.when(i == 0)
    def _first():
        for piece in merge_pieces:
            piece()
        store_merged_tile()

    @pl.when((i > 0) & (i < last))
    def _steady():
        ffn_stage(dict(zip((0, 1, 2, 3, 5), merge_pieces)))
        store_merged_tile()

    @pl.when(i == last)
    def _last():
        ffn_stage()


def _dot_nt(a, b):
    return lax.dot_general(a, b, (((1,), (1,)), ((), ())), preferred_element_type=F32)


def _attn_kernel(sinks_ref, qkva_ref, qkvb_ref, tab_ref, bias_b_ref, wg2_ref, wu2_ref, wd2_ref,
                 ya_ref, yb_ref, wg2_bf16_ref, wu2_bf16_ref, wd2_bf16_ref,
                 bias_a_ref, vext_a_ref, vext_b_ref, kdup_ref, s_ref):
    seq = qkva_ref.shape[0]
    n_pairs = A_WIDTH // LANES
    low = lax.broadcasted_iota(jnp.int32, (1, LANES), 1) < D_HEAD
    ones = jnp.ones((seq, LANES), BF16)
    for src, dst in ((wg2_ref, wg2_bf16_ref), (wu2_ref, wu2_bf16_ref), (wd2_ref, wd2_bf16_ref)):
        dst[...] = src[...].astype(BF16)

    @pl.when(pl.program_id(0) == 0)
    def _build_bias_a():
        qc = lax.broadcasted_iota(jnp.int32, (A_QBLOCK, A_QBLOCK), 0) // CHUNK
        kc = lax.broadcasted_iota(jnp.int32, (A_QBLOCK, A_QBLOCK), 1) // CHUNK
        for h in range(A_HEADS):
            for d in range(A_KEY_TILES):
                row = jnp.broadcast_to(tab_ref[h, d], (A_QBLOCK, 2 * A_QBLOCK))
                toeplitz = pltpu.roll(row, 0, 1, stride=1, stride_axis=0)[:, :A_QBLOCK]
                chunk_diff = d * (A_QBLOCK // CHUNK) + qc - kc
                visible = (chunk_diff >= 0) & (chunk_diff <= A_PREV_CHUNKS)
                bias_a_ref[h // 2, d, (h % 2) * A_QBLOCK:(h % 2 + 1) * A_QBLOCK, :] = jnp.where(
                    visible, toeplitz, NEG_INF)

    def mixer_a_scores(slot, j, m):
        q = qkva_ref[m * A_QBLOCK:(m + 1) * A_QBLOCK, j * LANES:(j + 1) * LANES]
        zero = jnp.zeros_like(q)
        qs = jnp.concatenate([jnp.where(low, q, zero), jnp.where(low, zero, q)], axis=0)
        kcols = slice(A_WIDTH + j * LANES, A_WIDTH + (j + 1) * LANES)
        dists = [d for d in range(A_KEY_TILES - 1, -1, -1) if m - d >= 0]
        key_rows = [slice((m - d) * A_QBLOCK, (m - d + 1) * A_QBLOCK) for d in dists]
        tile_max = []
        for t, (d, rows) in enumerate(zip(dists, key_rows)):
            logits = _dot_nt(qs, qkva_ref[rows, kcols]) + bias_a_ref[j, d]
            s_ref[slot, t] = logits
            tile_max.append(jnp.max(logits, axis=-1, keepdims=True))
        return functools.reduce(jnp.maximum, tile_max), key_rows

    def mixer_a_finish(slot, j, m, staged):
        mx, key_rows = staged
        acc = None
        for t, rows in enumerate(key_rows):
            pv = jnp.dot(jnp.exp2(s_ref[slot, t] - mx).astype(BF16),
                         vext_a_ref[rows, 2 * j * LANES:2 * (j + 1) * LANES],
                         preferred_element_type=F32)
            acc = pv if acc is None else acc + pv
        o = acc[:, :LANES] / acc[:, LANES:]
        ya_ref[m * A_QBLOCK:(m + 1) * A_QBLOCK, j * LANES:(j + 1) * LANES] = jnp.where(
            low, o[:A_QBLOCK], o[A_QBLOCK:]).astype(BF16)

    def mixer_b_scores(slot, m):
        krows = slice(max(m - 1, 0) * B_QBLOCK, (m + 1) * B_QBLOCK)
        width = krows.stop - krows.start
        maxima = []
        for g in range(B_KV_HEADS):
            tiles = [qkvb_ref[m * B_QBLOCK:(m + 1) * B_QBLOCK, c * LANES:(c + 1) * LANES]
                     for c in range(g * B_GROUP // 2, (g + 1) * B_GROUP // 2)]
            zero = jnp.zeros_like(tiles[0])
            qs = jnp.concatenate([piece for t in tiles
                                  for piece in (jnp.where(low, t, zero), jnp.where(low, zero, t))],
                                 axis=0)
            logits = _dot_nt(qs, kdup_ref[g, krows, :]) + bias_b_ref[g, :, B_WINDOW - width:]
            s_ref[slot, g, :, :width] = logits
            maxima.append(jnp.maximum(jnp.max(logits, axis=-1, keepdims=True), sink[g]))
        return maxima

    def mixer_b_finish(slot, m, staged):
        qrows = slice(m * B_QBLOCK, (m + 1) * B_QBLOCK)
        krows = slice(max(m - 1, 0) * B_QBLOCK, (m + 1) * B_QBLOCK)
        width = krows.stop - krows.start
        for g, mx in enumerate(staged):
            p = jnp.exp2(s_ref[slot, g, :, :width] - jnp.concatenate([mx] * (width // LANES), axis=1))
            pv = jnp.dot(p.astype(BF16), vext_b_ref[g, krows, :], preferred_element_type=F32)
            o = pv[:, :LANES] / (pv[:, LANES:] + jnp.exp2(sink[g] - mx))
            for t in range(B_GROUP // 2):
                c = g * B_GROUP // 2 + t
                even, odd = (o[(2 * t + h) * B_QBLOCK:(2 * t + h + 1) * B_QBLOCK] for h in (0, 1))
                yb_ref[qrows, c * LANES:(c + 1) * LANES] = jnp.where(low, even, odd).astype(BF16)

    for j in range(n_pairs):
        vcols = slice(2 * A_WIDTH + j * LANES, 2 * A_WIDTH + (j + 1) * LANES)
        vext_a_ref[:, 2 * j * LANES:(2 * j + 1) * LANES] = qkva_ref[:, vcols]
        vext_a_ref[:, (2 * j + 1) * LANES:2 * (j + 1) * LANES] = ones
    for g in range(B_KV_HEADS):
        for src, dst in ((qkvb_ref[:, B_Q_WIDTH:B_Q_WIDTH + LANES], kdup_ref.at[g]),
                         (qkvb_ref[:, B_Q_WIDTH + LANES:], vext_b_ref.at[g])):
            half = src[:, g * D_HEAD:(g + 1) * D_HEAD]
            dst[:, :LANES] = jnp.concatenate([half, half], axis=1)
        vext_b_ref[g, :, LANES:] = ones
    sink = [jnp.concatenate([jnp.full((B_QBLOCK, LANES), sinks_ref[g * B_GROUP + c], F32)
                             for c in range(B_GROUP)], axis=0) for g in range(B_KV_HEADS)]

    b_blocks_per_pair = seq // B_QBLOCK // n_pairs
    blocks = []
    for j in range(n_pairs):
        blocks += [(functools.partial(mixer_a_scores, j=j, m=m),
                    functools.partial(mixer_a_finish, j=j, m=m)) for m in range(seq // A_QBLOCK)]
        blocks += [(functools.partial(mixer_b_scores, m=m), functools.partial(mixer_b_finish, m=m))
                   for m in range(j * b_blocks_per_pair, (j + 1) * b_blocks_per_pair)]
    staged = blocks[0][0](slot=0)
    for n, (_, finish) in enumerate(blocks):
        next_staged = blocks[n + 1][0](slot=(n + 1) % 2) if n + 1 < len(blocks) else None
        finish(slot=n % 2, staged=staged)
        staged = next_staged


def _resident(shape):
    return pl.BlockSpec(shape, lambda *_: (0,) * len(shape), pipeline_mode=pl.Buffered(1))


def _bias_a_table_index():
    u = np.arange(2 * A_QBLOCK)
    key_minus_query = np.where(u <= A_QBLOCK, u, u - 2 * A_QBLOCK)
    rel = np.arange(A_KEY_TILES)[:, None] * A_QBLOCK - key_minus_query[None, :]
    return np.clip(rel, -(CHUNK - 1), MAX_REL) + (CHUNK - 1)


def _bias_b_table():
    slopes = np.array([2.0 ** (-8.0 * (h + 1) / B_Q_HEADS) for h in range(B_Q_HEADS)], np.float32)
    i = np.arange(B_QBLOCK)[:, None]
    j = np.arange(B_WINDOW)[None, :]
    rel = i - j + B_QBLOCK
    chunk_diff = i // CHUNK - j // CHUNK + B_QBLOCK // CHUNK
    vis = (chunk_diff >= 0) & (chunk_diff <= B_PREV_CHUNKS)
    dist = np.abs(rel).astype(np.float32)
    out = np.empty((B_KV_HEADS, B_GROUP * B_QBLOCK, B_WINDOW), np.float32)
    for g in range(B_KV_HEADS):
        for c in range(B_GROUP):
            out[g, c * B_QBLOCK:(c + 1) * B_QBLOCK] = np.where(
                vis, -slopes[g * B_GROUP + c] * dist * LOG2_E, NEG_INF)
    return out


def kernel(x, ffn1_norm, ffn1_w_gate, ffn1_w_up, ffn1_w_down, mix_norm, w_in, rel_bias, sinks,
           w_proj_a, w_proj_b, w_out, ffn2_norm, ffn2_w_gate, ffn2_w_up, ffn2_w_down, final_norm):
    batch, seq, d = x.shape
    rows = batch * seq
    assert d == D_MODEL and rows % ROW_TILE == 0 and seq % A_QBLOCK == 0
    n_row_tiles = rows // ROW_TILE
    vec = lambda g: g.reshape(1, D_MODEL).astype(F32)

    dense_params = pltpu.CompilerParams(dimension_semantics=("arbitrary",),
                                        vmem_limit_bytes=VMEM_LIMIT_BYTES)

    stage0_tile = lambda width, col=0: pl.BlockSpec(
        (ROW_TILE, width), lambda i: (jnp.minimum(i, n_row_tiles - 1), col))
    stage1_tile = lambda width: pl.BlockSpec(
        (ROW_TILE, width), lambda i: (jnp.maximum(i - 1, 0), 0))
    x1, qkva, qkvb, gates = pl.pallas_call(
        _ffn_in_kernel,
        grid=(n_row_tiles + 1,),
        in_specs=[stage0_tile(D_MODEL), _resident((1, D_MODEL)),
                  _resident((D_MODEL, D_FF)), _resident((D_MODEL, D_FF)),
                  _resident((D_FF, D_MODEL)), _resident((1, D_MODEL)),
                  _resident(w_in.shape)],
        out_specs=[stage0_tile(D_MODEL), stage1_tile(QKV_A_WIDTH), stage1_tile(QKV_B_WIDTH),
                   stage1_tile(GATES_WIDTH)],
        out_shape=[jax.ShapeDtypeStruct((rows, D_MODEL), F32),
                   jax.ShapeDtypeStruct((rows, QKV_A_WIDTH), BF16),
                   jax.ShapeDtypeStruct((rows, QKV_B_WIDTH), BF16),
                   jax.ShapeDtypeStruct((rows, GATES_WIDTH), BF16)],
        scratch_shapes=[pltpu.VMEM((ROW_TILE, D_FF), BF16), pltpu.VMEM((ROW_TILE, D_MODEL), BF16)],
        compiler_params=dense_params,
        name="ffn_in",
    )(x.reshape(rows, D_MODEL), vec(ffn1_norm), ffn1_w_gate.astype(BF16),
      ffn1_w_up.astype(BF16), ffn1_w_down.astype(BF16), vec(mix_norm), w_in.astype(BF16))

    tab_a = (rel_bias.astype(F32) * LOG2_E)[:, _bias_a_table_index()][:, :, None, :]
    n_pairs = A_WIDTH // LANES
    batch_block = lambda width: pl.BlockSpec((None, seq, width), lambda b: (b, 0, 0))
    ffn2_weights = (ffn2_w_gate, ffn2_w_up, ffn2_w_down)
    assert all(w.shape[0] % (batch * BF16_SUBLANES) == 0 for w in ffn2_weights)
    cast_block = lambda w: pl.BlockSpec((w.shape[0] // batch, w.shape[1]), lambda b: (b, 0))
    ya, yb, wg2, wu2, wd2 = pl.pallas_call(
        _attn_kernel,
        grid=(batch,),
        in_specs=[pl.BlockSpec(memory_space=pltpu.SMEM),
                  batch_block(QKV_A_WIDTH), batch_block(QKV_B_WIDTH),
                  _resident((A_HEADS, A_KEY_TILES, 1, 2 * A_QBLOCK)),
                  _resident((B_KV_HEADS, B_GROUP * B_QBLOCK, B_WINDOW))]
                 + [cast_block(w) for w in ffn2_weights],
        out_specs=[batch_block(A_WIDTH), batch_block(B_Q_WIDTH)]
                  + [cast_block(w) for w in ffn2_weights],
        out_shape=[jax.ShapeDtypeStruct((batch, seq, A_WIDTH), BF16),
                   jax.ShapeDtypeStruct((batch, seq, B_Q_WIDTH), BF16)]
                  + [jax.ShapeDtypeStruct(w.shape, BF16) for w in ffn2_weights],
        scratch_shapes=[pltpu.VMEM((n_pairs, A_KEY_TILES, 2 * A_QBLOCK, A_QBLOCK), F32),
                        pltpu.VMEM((seq, 2 * A_WIDTH), BF16),
                        pltpu.VMEM((B_KV_HEADS, seq, 2 * LANES), BF16),
                        pltpu.VMEM((B_KV_HEADS, seq, LANES), BF16),
                        pltpu.VMEM((2, A_KEY_TILES, 2 * A_QBLOCK, A_QBLOCK), F32)],
        compiler_params=pltpu.CompilerParams(dimension_semantics=("arbitrary",),
                                             vmem_limit_bytes=VMEM_LIMIT_BYTES),
        name="attn",
    )(sinks.astype(F32) * LOG2_E, qkva.reshape(batch, seq, QKV_A_WIDTH),
      qkvb.reshape(batch, seq, QKV_B_WIDTH), tab_a, jnp.asarray(_bias_b_table()), *ffn2_weights)

    out = pl.pallas_call(
        _ffn_out_kernel,
        grid=(n_row_tiles + 1,),
        in_specs=[stage0_tile(D_MODEL), stage0_tile(A_WIDTH), stage0_tile(B_Q_WIDTH),
                  stage0_tile(D_MODEL, col=0), stage0_tile(D_MODEL, col=1),
                  _resident((A_WIDTH, D_MODEL)), _resident((B_Q_WIDTH, D_MODEL)),
                  _resident((D_MODEL, D_MODEL)), _resident((1, D_MODEL)),
                  _resident((D_MODEL, D_FF)), _resident((D_MODEL, D_FF)),
                  _resident((D_FF, D_MODEL)), _resident((1, D_MODEL))],
        out_specs=stage1_tile(D_MODEL),
        out_shape=jax.ShapeDtypeStruct((rows, D_MODEL), F32),
        scratch_shapes=[pltpu.VMEM((ROW_TILE, D_FF), BF16), pltpu.VMEM((ROW_TILE, D_MODEL), F32),
                        pltpu.VMEM((ROW_TILE, D_MODEL), BF16), pltpu.VMEM((ROW_TILE, D_MODEL), BF16)],
        compiler_params=dense_params,
        name="ffn_out",
    )(x1, ya.reshape(rows, A_WIDTH), yb.reshape(rows, B_Q_WIDTH), gates, gates,
      w_proj_a.astype(BF16), w_proj_b.astype(BF16), w_out.astype(BF16), vec(ffn2_norm),
      wg2, wu2, wd2, vec(final_norm))
    return out.reshape(batch, seq, D_MODEL)
```

```python
import functools

import numpy as np
import jax
import jax.numpy as jnp
from jax import lax
from jax.experimental import pallas as pl
from jax.experimental.pallas import tpu as pltpu

F32 = jnp.float32
BF16 = jnp.bfloat16

D_MODEL = 1024
D_FF = 2816
CHUNK = 64
D_HEAD = 64
A_HEADS = 8
A_PREV_CHUNKS = 8
MAX_REL = 128
B_Q_HEADS = 8
B_KV_HEADS = 2
B_GROUP = B_Q_HEADS // B_KV_HEADS
B_PREV_CHUNKS = 2
A_WIDTH = A_HEADS * D_HEAD
B_Q_WIDTH = B_Q_HEADS * D_HEAD
B_KV_WIDTH = B_KV_HEADS * D_HEAD
QKV_A_WIDTH = 3 * A_WIDTH
QKV_B_WIDTH = B_Q_WIDTH + 2 * B_KV_WIDTH
GATES_WIDTH = 2 * D_MODEL
EPS = 1e-6
NEG_INF = -1e30
LOG2_E = float(np.log2(np.e))
QK_SCALE = LOG2_E / float(np.sqrt(D_HEAD))

LANES = 128
BF16_SUBLANES = 16
MXU_DIM = 256
VMEM_LIMIT_BYTES = 56 * 1024 * 1024

ROW_TILE = 512
FF_CHUNK = MXU_DIM
A_QBLOCK = 4 * CHUNK
A_KEY_TILES = A_PREV_CHUNKS * CHUNK // A_QBLOCK + 1
B_QBLOCK = 2 * CHUNK
B_WINDOW = 2 * B_QBLOCK
assert B_GROUP * B_QBLOCK == 2 * A_QBLOCK and B_WINDOW == A_QBLOCK and B_KV_HEADS <= A_KEY_TILES


def _rms_norm(x, gain):
    return x * lax.rsqrt(jnp.mean(x * x, axis=-1, keepdims=True) + EPS) * gain


def _swiglu(h, wg_ref, wu_ref, wd_ref, hid_ref, between_chunks=None):
    between_chunks = between_chunks or {}
    for c in range(D_FF // FF_CHUNK):
        cols = slice(c * FF_CHUNK, (c + 1) * FF_CHUNK)
        g = jnp.dot(h, wg_ref[:, cols], preferred_element_type=F32)
        u = jnp.dot(h, wu_ref[:, cols], preferred_element_type=F32)
        hid_ref[:, cols] = (g * jax.nn.sigmoid(g) * u).astype(BF16)
        if c in between_chunks:
            between_chunks[c]()
    return jnp.dot(hid_ref[...], wd_ref[...], preferred_element_type=F32)


def _ffn_in_kernel(x_ref, g1_ref, wg_ref, wu_ref, wd_ref, gm_ref, win_ref,
                   x1_ref, qkva_ref, qkvb_ref, gates_ref, hid_ref, h2_ref):
    i = pl.program_id(0)
    last = pl.num_programs(0) - 1

    def ffn_stage():
        x = x_ref[...]
        h = _rms_norm(x, g1_ref[...]).astype(BF16)
        x1 = x + 0.5 * _swiglu(h, wg_ref, wu_ref, wd_ref, hid_ref)
        x1_ref[...] = x1
        h2_ref[...] = _rms_norm(x1, gm_ref[...]).astype(BF16)

    def projection_stage():
        h2 = h2_ref[...]
        col = 0
        for o_ref, q_width in ((qkva_ref, A_WIDTH), (qkvb_ref, B_Q_WIDTH), (gates_ref, 0)):
            for lo, hi, factor in ((0, q_width, QK_SCALE), (q_width, o_ref.shape[1], None)):
                if hi > lo:
                    y = jnp.dot(h2, win_ref[:, col + lo:col + hi], preferred_element_type=F32)
                    o_ref[:, lo:hi] = (y if factor is None else y * factor).astype(BF16)
            col += o_ref.shape[1]

    @pl.when(i == 0)
    def _first():
        ffn_stage()

    @pl.when((i > 0) & (i < last))
    def _steady():
        projection_stage()
        ffn_stage()

    @pl.when(i == last)
    def _last():
        projection_stage()


def _ffn_out_kernel(x1_ref, ya_ref, yb_ref, ga_ref, gb_ref, wpa_ref, wpb_ref, wo_ref,
                    g2_ref, wg_ref, wu_ref, wd_ref, gf_ref, out_ref, hid_ref, x2_ref, h_ref,
                    mg_ref):
    i = pl.program_id(0)
    last = pl.num_programs(0) - 1
    merged_tile = {}

    def gate_columns(c):
        cols = slice(c * MXU_DIM, (c + 1) * MXU_DIM)
        pa = jnp.dot(ya_ref[...], wpa_ref[:, cols], preferred_element_type=F32)
        pb = jnp.dot(yb_ref[...], wpb_ref[:, cols], preferred_element_type=F32)
        mg_ref[:, cols] = (jax.nn.sigmoid(ga_ref[:, cols].astype(F32)) * pa
                           + jax.nn.sigmoid(gb_ref[:, cols].astype(F32)) * pb).astype(BF16)

    def mix_in():
        x2 = x1_ref[...] + jnp.dot(mg_ref[...], wo_ref[...], preferred_element_type=F32)
        merged_tile["x2"] = x2
        merged_tile["h"] = _rms_norm(x2, g2_ref[...]).astype(BF16)

    def store_merged_tile():
        x2_ref[...] = merged_tile["x2"]
        h_ref[...] = merged_tile["h"]

    merge_pieces = [functools.partial(gate_columns, c) for c in range(D_MODEL // MXU_DIM)]
    merge_pieces.append(mix_in)

    def ffn_stage(between_chunks=None):
        x3 = x2_ref[...] + 0.5 * _swiglu(h_ref[...], wg_ref, wu_ref, wd_ref, hid_ref,
                                         between_chunks)
        out_ref[...] = _rms_norm(x3, gf_ref[...])

    @pl.when(i == 0)
    def _first():
        for piece in merge_pieces:
            piece()
        store_merged_tile()

    @pl.when((i > 0) & (i < last))
    def _steady():
        ffn_stage(dict(zip((0, 1, 2, 3, 5), merge_pieces)))
        store_merged_tile()

    @pl.when(i == last)
    def _last():
        ffn_stage()


def _dot_nt(a, b):
    return lax.dot_general(a, b, (((1,), (1,)), ((), ())), preferred_element_type=F32)


def _attn_kernel(sinks_ref, qkva_ref, qkvb_ref, tab_ref, bias_b_ref, wg2_ref, wu2_ref, wd2_ref,
                 ya_ref, yb_ref, wg2_bf16_ref, wu2_bf16_ref, wd2_bf16_ref,
                 bias_a_ref, vext_b_ref, kdup_ref, s_ref):
    seq = qkva_ref.shape[0]
    n_pairs = A_WIDTH // LANES
    low = lax.broadcasted_iota(jnp.int32, (1, LANES), 1) < D_HEAD
    ones = jnp.ones((seq, LANES), BF16)
    for src, dst in ((wg2_ref, wg2_bf16_ref), (wu2_ref, wu2_bf16_ref), (wd2_ref, wd2_bf16_ref)):
        dst[...] = src[...].astype(BF16)

    @pl.when(pl.program_id(0) == 0)
    def _build_bias_a():
        qc = lax.broadcasted_iota(jnp.int32, (A_QBLOCK, A_QBLOCK), 0) // CHUNK
        kc = lax.broadcasted_iota(jnp.int32, (A_QBLOCK, A_QBLOCK), 1) // CHUNK
        for h in range(A_HEADS):
            for d in range(A_KEY_TILES):
                row = jnp.broadcast_to(tab_ref[h, d], (A_QBLOCK, 2 * A_QBLOCK))
                toeplitz = pltpu.roll(row, 0, 1, stride=1, stride_axis=0)[:, :A_QBLOCK]
                chunk_diff = d * (A_QBLOCK // CHUNK) + qc - kc
                visible = (chunk_diff >= 0) & (chunk_diff <= A_PREV_CHUNKS)
                bias_a_ref[h // 2, d, (h % 2) * A_QBLOCK:(h % 2 + 1) * A_QBLOCK, :] = jnp.where(
                    visible, toeplitz, NEG_INF)

    def mixer_a_scores(slot, j, m):
        q = qkva_ref[m * A_QBLOCK:(m + 1) * A_QBLOCK, j * LANES:(j + 1) * LANES]
        zero = jnp.zeros_like(q)
        qs = jnp.concatenate([jnp.where(low, q, zero), jnp.where(low, zero, q)], axis=0)
        kcols = slice(A_WIDTH + j * LANES, A_WIDTH + (j + 1) * LANES)
        dists = [d for d in range(A_KEY_TILES - 1, -1, -1) if m - d >= 0]
        key_rows = [slice((m - d) * A_QBLOCK, (m - d + 1) * A_QBLOCK) for d in dists]
        tile_max = []
        for t, (d, rows) in enumerate(zip(dists, key_rows)):
            logits = _dot_nt(qs, qkva_ref[rows, kcols]) + bias_a_ref[j, d]
            s_ref[slot, t] = logits
            tile_max.append(jnp.max(logits, axis=-1, keepdims=True))
        return functools.reduce(jnp.maximum, tile_max), key_rows

    def mixer_a_finish(slot, j, m, staged):
        mx, key_rows = staged
        acc = None
        vcols = slice(2 * A_WIDTH + j * LANES, 2 * A_WIDTH + (j + 1) * LANES)
        for t, rows in enumerate(key_rows):
            v_and_ones = jnp.concatenate([qkva_ref[rows, vcols], ones[:A_QBLOCK]], axis=1)
            pv = jnp.dot(jnp.exp2(s_ref[slot, t] - mx).astype(BF16), v_and_ones,
                         preferred_element_type=F32)
            acc = pv if acc is None else acc + pv
        o = acc[:, :LANES] / acc[:, LANES:]
        ya_ref[m * A_QBLOCK:(m + 1) * A_QBLOCK, j * LANES:(j + 1) * LANES] = jnp.where(
            low, o[:A_QBLOCK], o[A_QBLOCK:]).astype(BF16)

    def mixer_b_scores(slot, m):
        krows = slice(max(m - 1, 0) * B_QBLOCK, (m + 1) * B_QBLOCK)
        width = krows.stop - krows.start
        maxima = []
        for g in range(B_KV_HEADS):
            tiles = [qkvb_ref[m * B_QBLOCK:(m + 1) * B_QBLOCK, c * LANES:(c + 1) * LANES]
                     for c in range(g * B_GROUP // 2, (g + 1) * B_GROUP // 2)]
            zero = jnp.zeros_like(tiles[0])
            qs = jnp.concatenate([piece for t in tiles
                                  for piece in (jnp.where(low, t, zero), jnp.where(low, zero, t))],
                                 axis=0)
            logits = _dot_nt(qs, kdup_ref[g, krows, :]) + bias_b_ref[g, :, B_WINDOW - width:]
            s_ref[slot, g, :, :width] = logits
            maxima.append(jnp.maximum(jnp.max(logits, axis=-1, keepdims=True), sink[g]))
        return maxima

    def mixer_b_finish(slot, m, staged):
        qrows = slice(m * B_QBLOCK, (m + 1) * B_QBLOCK)
        krows = slice(max(m - 1, 0) * B_QBLOCK, (m + 1) * B_QBLOCK)
        width = krows.stop - krows.start
        for g, mx in enumerate(staged):
            p = jnp.exp2(s_ref[slot, g, :, :width] - jnp.concatenate([mx] * (width // LANES), axis=1))
            pv = jnp.dot(p.astype(BF16), vext_b_ref[g, krows, :], preferred_element_type=F32)
            o = pv[:, :LANES] / (pv[:, LANES:] + jnp.exp2(sink[g] - mx))
            for t in range(B_GROUP // 2):
                c = g * B_GROUP // 2 + t
                even, odd = (o[(2 * t + h) * B_QBLOCK:(2 * t + h + 1) * B_QBLOCK] for h in (0, 1))
                yb_ref[qrows, c * LANES:(c + 1) * LANES] = jnp.where(low, even, odd).astype(BF16)

    for g in range(B_KV_HEADS):
        for src, dst in ((qkvb_ref[:, B_Q_WIDTH:B_Q_WIDTH + LANES], kdup_ref.at[g]),
                         (qkvb_ref[:, B_Q_WIDTH + LANES:], vext_b_ref.at[g])):
            half = src[:, g * D_HEAD:(g + 1) * D_HEAD]
            dst[:, :LANES] = jnp.concatenate([half, half], axis=1)
        vext_b_ref[g, :, LANES:] = ones
    sink = [jnp.concatenate([jnp.full((B_QBLOCK, LANES), sinks_ref[g * B_GROUP + c], F32)
                             for c in range(B_GROUP)], axis=0) for g in range(B_KV_HEADS)]

    b_blocks_per_pair = seq // B_QBLOCK // n_pairs
    blocks = []
    for j in range(n_pairs):
        blocks += [(functools.partial(mixer_a_scores, j=j, m=m),
                    functools.partial(mixer_a_finish, j=j, m=m)) for m in range(seq // A_QBLOCK)]
        blocks += [(functools.partial(mixer_b_scores, m=m), functools.partial(mixer_b_finish, m=m))
                   for m in range(j * b_blocks_per_pair, (j + 1) * b_blocks_per_pair)]
    staged = blocks[0][0](slot=0)
    for n, (_, finish) in enumerate(blocks):
        next_staged = blocks[n + 1][0](slot=(n + 1) % 2) if n + 1 < len(blocks) else None
        finish(slot=n % 2, staged=staged)
        staged = next_staged


def _resident(shape):
    return pl.BlockSpec(shape, lambda *_: (0,) * len(shape), pipeline_mode=pl.Buffered(1))


def _bias_a_table_index():
    u = np.arange(2 * A_QBLOCK)
    key_minus_query = np.where(u <= A_QBLOCK, u, u - 2 * A_QBLOCK)
    rel = np.arange(A_KEY_TILES)[:, None] * A_QBLOCK - key_minus_query[None, :]
    return np.clip(rel, -(CHUNK - 1), MAX_REL) + (CHUNK - 1)


def _bias_b_table():
    slopes = np.array([2.0 ** (-8.0 * (h + 1) / B_Q_HEADS) for h in range(B_Q_HEADS)], np.float32)
    i = np.arange(B_QBLOCK)[:, None]
    j = np.arange(B_WINDOW)[None, :]
    rel = i - j + B_QBLOCK
    chunk_diff = i // CHUNK - j // CHUNK + B_QBLOCK // CHUNK
    vis = (chunk_diff >= 0) & (chunk_diff <= B_PREV_CHUNKS)
    dist = np.abs(rel).astype(np.float32)
    out = np.empty((B_KV_HEADS, B_GROUP * B_QBLOCK, B_WINDOW), np.float32)
    for g in range(B_KV_HEADS):
        for c in range(B_GROUP):
            out[g, c * B_QBLOCK:(c + 1) * B_QBLOCK] = np.where(
                vis, -slopes[g * B_GROUP + c] * dist * LOG2_E, NEG_INF)
    return out


def kernel(x, ffn1_norm, ffn1_w_gate, ffn1_w_up, ffn1_w_down, mix_norm, w_in, rel_bias, sinks,
           w_proj_a, w_proj_b, w_out, ffn2_norm, ffn2_w_gate, ffn2_w_up, ffn2_w_down, final_norm):
    batch, seq, d = x.shape
    rows = batch * seq
    assert d == D_MODEL and rows % ROW_TILE == 0 and seq % A_QBLOCK == 0
    n_row_tiles = rows // ROW_TILE
    vec = lambda g: g.reshape(1, D_MODEL).astype(F32)

    dense_params = pltpu.CompilerParams(dimension_semantics=("arbitrary",),
                                        vmem_limit_bytes=VMEM_LIMIT_BYTES)

    stage0_tile = lambda width, col=0: pl.BlockSpec(
        (ROW_TILE, width), lambda i: (jnp.minimum(i, n_row_tiles - 1), col))
    stage1_tile = lambda width: pl.BlockSpec(
        (ROW_TILE, width), lambda i: (jnp.maximum(i - 1, 0), 0))
    x1, qkva, qkvb, gates = pl.pallas_call(
        _ffn_in_kernel,
        grid=(n_row_tiles + 1,),
        in_specs=[stage0_tile(D_MODEL), _resident((1, D_MODEL)),
                  _resident((D_MODEL, D_FF)), _resident((D_MODEL, D_FF)),
                  _resident((D_FF, D_MODEL)), _resident((1, D_MODEL)),
                  _resident(w_in.shape)],
        out_specs=[stage0_tile(D_MODEL), stage1_tile(QKV_A_WIDTH), stage1_tile(QKV_B_WIDTH),
                   stage1_tile(GATES_WIDTH)],
        out_shape=[jax.ShapeDtypeStruct((rows, D_MODEL), F32),
                   jax.ShapeDtypeStruct((rows, QKV_A_WIDTH), BF16),
                   jax.ShapeDtypeStruct((rows, QKV_B_WIDTH), BF16),
                   jax.ShapeDtypeStruct((rows, GATES_WIDTH), BF16)],
        scratch_shapes=[pltpu.VMEM((ROW_TILE, D_FF), BF16), pltpu.VMEM((ROW_TILE, D_MODEL), BF16)],
        compiler_params=dense_params,
        name="ffn_in",
    )(x.reshape(rows, D_MODEL), vec(ffn1_norm), ffn1_w_gate.astype(BF16),
      ffn1_w_up.astype(BF16), ffn1_w_down.astype(BF16), vec(mix_norm), w_in.astype(BF16))

    tab_a = (rel_bias.astype(F32) * LOG2_E)[:, _bias_a_table_index()][:, :, None, :]
    n_pairs = A_WIDTH // LANES
    batch_block = lambda width: pl.BlockSpec((None, seq, width), lambda b: (b, 0, 0))
    ffn2_weights = (ffn2_w_gate, ffn2_w_up, ffn2_w_down)
    assert all(w.shape[0] % (batch * BF16_SUBLANES) == 0 for w in ffn2_weights)
    cast_block = lambda w: pl.BlockSpec((w.shape[0] // batch, w.shape[1]), lambda b: (b, 0))
    ya, yb, wg2, wu2, wd2 = pl.pallas_call(
        _attn_kernel,
        grid=(batch,),
        in_specs=[pl.BlockSpec(memory_space=pltpu.SMEM),
                  batch_block(QKV_A_WIDTH), batch_block(QKV_B_WIDTH),
                  _resident((A_HEADS, A_KEY_TILES, 1, 2 * A_QBLOCK)),
                  _resident((B_KV_HEADS, B_GROUP * B_QBLOCK, B_WINDOW))]
                 + [cast_block(w) for w in ffn2_weights],
        out_specs=[batch_block(A_WIDTH), batch_block(B_Q_WIDTH)]
                  + [cast_block(w) for w in ffn2_weights],
        out_shape=[jax.ShapeDtypeStruct((batch, seq, A_WIDTH), BF16),
                   jax.ShapeDtypeStruct((batch, seq, B_Q_WIDTH), BF16)]
                  + [jax.ShapeDtypeStruct(w.shape, BF16) for w in ffn2_weights],
        scratch_shapes=[pltpu.VMEM((n_pairs, A_KEY_TILES, 2 * A_QBLOCK, A_QBLOCK), F32),
                        pltpu.VMEM((B_KV_HEADS, seq, 2 * LANES), BF16),
                        pltpu.VMEM((B_KV_HEADS, seq, LANES), BF16),
                        pltpu.VMEM((2, A_KEY_TILES, 2 * A_QBLOCK, A_QBLOCK), F32)],
        compiler_params=pltpu.CompilerParams(dimension_semantics=("arbitrary",),
                                             vmem_limit_bytes=VMEM_LIMIT_BYTES),
        name="attn",
    )(sinks.astype(F32) * LOG2_E, qkva.reshape(batch, seq, QKV_A_WIDTH),
      qkvb.reshape(batch, seq, QKV_B_WIDTH), tab_a, jnp.asarray(_bias_b_table()), *ffn2_weights)

    out = pl.pallas_call(
        _ffn_out_kernel,
        grid=(n_row_tiles + 1,),
        in_specs=[stage0_tile(D_MODEL), stage0_tile(A_WIDTH), stage0_tile(B_Q_WIDTH),
                  stage0_tile(D_MODEL, col=0), stage0_tile(D_MODEL, col=1),
                  _resident((A_WIDTH, D_MODEL)), _resident((B_Q_WIDTH, D_MODEL)),
                  _resident((D_MODEL, D_MODEL)), _resident((1, D_MODEL)),
                  _resident((D_MODEL, D_FF)), _resident((D_MODEL, D_FF)),
                  _resident((D_FF, D_MODEL)), _resident((1, D_MODEL))],
        out_specs=stage1_tile(D_MODEL),
        out_shape=jax.ShapeDtypeStruct((rows, D_MODEL), F32),
        scratch_shapes=[pltpu.VMEM((ROW_TILE, D_FF), BF16), pltpu.VMEM((ROW_TILE, D_MODEL), F32),
                        pltpu.VMEM((ROW_TILE, D_MODEL), BF16), pltpu.VMEM((ROW_TILE, D_MODEL), BF16)],
        compiler_params=dense_params,
        name="ffn_out",
    )(x1, ya.reshape(rows, A_WIDTH), yb.reshape(rows, B_Q_WIDTH), gates, gates,
      w_proj_a.astype(BF16), w_proj_b.astype(BF16), w_out.astype(BF16), vec(ffn2_norm),
      wg2, wu2, wd2, vec(final_norm))
    return out.reshape(batch, seq, D_MODEL)
```

```python
import functools

import numpy as np
import jax
import jax.numpy as jnp
from jax import lax
from jax.experimental import pallas as pl
from jax.experimental.pallas import tpu as pltpu

F32 = jnp.float32
BF16 = jnp.bfloat16

D_MODEL = 1024
D_FF = 2816
CHUNK = 64
D_HEAD = 64
A_HEADS = 8
A_PREV_CHUNKS = 8
MAX_REL = 128
B_Q_HEADS = 8
B_KV_HEADS = 2
B_GROUP = B_Q_HEADS // B_KV_HEADS
B_PREV_CHUNKS = 2
A_WIDTH = A_HEADS * D_HEAD
B_Q_WIDTH = B_Q_HEADS * D_HEAD
B_KV_WIDTH = B_KV_HEADS * D_HEAD
QKV_A_WIDTH = 3 * A_WIDTH
QKV_B_WIDTH = B_Q_WIDTH + 2 * B_KV_WIDTH
GATES_WIDTH = 2 * D_MODEL
EPS = 1e-6
NEG_INF = -1e30
LOG2_E = float(np.log2(np.e))
QK_SCALE = LOG2_E / float(np.sqrt(D_HEAD))

LANES = 128
BF16_SUBLANES = 16
MXU_DIM = 256
VMEM_LIMIT_BYTES = 56 * 1024 * 1024

ROW_TILE = 512
FF_CHUNK = MXU_DIM
A_QBLOCK = 4 * CHUNK
A_KEY_TILES = A_PREV_CHUNKS * CHUNK // A_QBLOCK + 1
B_QBLOCK = 2 * CHUNK
B_WINDOW = 2 * B_QBLOCK
assert B_GROUP * B_QBLOCK == 2 * A_QBLOCK and B_WINDOW == A_QBLOCK and B_KV_HEADS <= A_KEY_TILES


def _rms_norm(x, gain):
    return x * lax.rsqrt(jnp.mean(x * x, axis=-1, keepdims=True) + EPS) * gain


def _swiglu(h, wg_ref, wu_ref, wd_ref, hid_ref, between_chunks=None):
    between_chunks = between_chunks or {}
    for c in range(D_FF // FF_CHUNK):
        cols = slice(c * FF_CHUNK, (c + 1) * FF_CHUNK)
        g = jnp.dot(h, wg_ref[:, cols], preferred_element_type=F32)
        u = jnp.dot(h, wu_ref[:, cols], preferred_element_type=F32)
        hid_ref[:, cols] = (g * jax.nn.sigmoid(g) * u).astype(BF16)
        if c in between_chunks:
            between_chunks[c]()
    return jnp.dot(hid_ref[...], wd_ref[...], preferred_element_type=F32)


def _ffn_in_kernel(x_ref, g1_ref, wg_ref, wu_ref, wd_ref, gm_ref, win_ref,
                   x1_ref, qkva_ref, qkvb_ref, gates_ref, hid_ref, h2_ref):
    i = pl.program_id(0)
    last = pl.num_programs(0) - 1

    def ffn_stage():
        x = x_ref[...]
        h = _rms_norm(x, g1_ref[...]).astype(BF16)
        x1 = x + 0.5 * _swiglu(h, wg_ref, wu_ref, wd_ref, hid_ref)
        x1_ref[...] = x1
        h2_ref[...] = _rms_norm(x1, gm_ref[...]).astype(BF16)

    def projection_stage():
        h2 = h2_ref[...]
        col = 0
        for o_ref, q_width in ((qkva_ref, A_WIDTH), (qkvb_ref, B_Q_WIDTH), (gates_ref, 0)):
            for lo, hi, factor in ((0, q_width, QK_SCALE), (q_width, o_ref.shape[1], None)):
                if hi > lo:
                    y = jnp.dot(h2, win_ref[:, col + lo:col + hi], preferred_element_type=F32)
                    o_ref[:, lo:hi] = (y if factor is None else y * factor).astype(BF16)
            col += o_ref.shape[1]

    @pl.when(i == 0)
    def _first():
        ffn_stage()

    @pl.when((i > 0) & (i < last))
    def _steady():
        projection_stage()
        ffn_stage()

    @pl.when(i == last)
    def _last():
        projection_stage()


def _ffn_out_kernel(x1_ref, ya_ref, yb_ref, ga_ref, gb_ref, wpa_ref, wpb_ref, wo_ref,
                    g2_ref, wg_ref, wu_ref, wd_ref, gf_ref, out_ref, hid_ref, x2_ref, h_ref,
                    mg_ref):
    i = pl.program_id(0)
    last = pl.num_programs(0) - 1
    merged_tile = {}

    def gate_columns(c):
        cols = slice(c * MXU_DIM, (c + 1) * MXU_DIM)
        pa = jnp.dot(ya_ref[...], wpa_ref[:, cols], preferred_element_type=F32)
        pb = jnp.dot(yb_ref[...], wpb_ref[:, cols], preferred_element_type=F32)
        mg_ref[:, cols] = (jax.nn.sigmoid(ga_ref[:, cols].astype(F32)) * pa
                           + jax.nn.sigmoid(gb_ref[:, cols].astype(F32)) * pb).astype(BF16)

    def mix_in():
        x2 = x1_ref[...] + jnp.dot(mg_ref[...], wo_ref[...], preferred_element_type=F32)
        merged_tile["x2"] = x2
        merged_tile["h"] = _rms_norm(x2, g2_ref[...]).astype(BF16)

    def store_merged_tile():
        x2_ref[...] = merged_tile["x2"]
        h_ref[...] = merged_tile["h"]

    merge_pieces = [functools.partial(gate_columns, c) for c in range(D_MODEL // MXU_DIM)]
    merge_pieces.append(mix_in)

    def ffn_stage(between_chunks=None):
        x3 = x2_ref[...] + 0.5 * _swiglu(h_ref[...], wg_ref, wu_ref, wd_ref, hid_ref,
                                         between_chunks)
        out_ref[...] = _rms_norm(x3, gf_ref[...])

    @pl.when(i == 0)
    def _first():
        for piece in merge_pieces:
            piece()
        store_merged_tile()

    @pl.when((i > 0) & (i < last))
    def _steady():
        ffn_stage(dict(zip((0, 1, 2, 3, 5), merge_pieces)))
        store_merged_tile()

    @pl.when(i == last)
    def _last():
        ffn_stage()


def _dot_nt(a, b):
    return lax.dot_general(a, b, (((1,), (1,)), ((), ())), preferred_element_type=F32)


def _attn_kernel(sinks_ref, qkva_ref, qkvb_ref, tab_ref, bias_b_ref, wg2_ref, wu2_ref, wd2_ref,
                 ya_ref, yb_ref, wg2_bf16_ref, wu2_bf16_ref, wd2_bf16_ref,
                 bias_a_ref, vext_b_ref, kdup_ref, s_ref):
    seq = qkva_ref.shape[0]
    n_pairs = A_WIDTH // LANES
    low = lax.broadcasted_iota(jnp.int32, (1, LANES), 1) < D_HEAD
    ones = jnp.ones((seq, LANES), BF16)
    for src, dst in ((wg2_ref, wg2_bf16_ref), (wu2_ref, wu2_bf16_ref), (wd2_ref, wd2_bf16_ref)):
        dst[...] = src[...].astype(BF16)

    @pl.when(pl.program_id(0) == 0)
    def _build_bias_a():
        qc = lax.broadcasted_iota(jnp.int32, (A_QBLOCK, A_QBLOCK), 0) // CHUNK
        kc = lax.broadcasted_iota(jnp.int32, (A_QBLOCK, A_QBLOCK), 1) // CHUNK
        for h in range(A_HEADS):
            for d in range(A_KEY_TILES):
                row = jnp.broadcast_to(tab_ref[h, d], (A_QBLOCK, 2 * A_QBLOCK))
                toeplitz = pltpu.roll(row, 0, 1, stride=1, stride_axis=0)[:, :A_QBLOCK]
                chunk_diff = d * (A_QBLOCK // CHUNK) + qc - kc
                visible = (chunk_diff >= 0) & (chunk_diff <= A_PREV_CHUNKS)
                bias_a_ref[h // 2, d, (h % 2) * A_QBLOCK:(h % 2 + 1) * A_QBLOCK, :] = jnp.where(
                    visible, toeplitz, NEG_INF)

    def mixer_a_scores(slot, j, m):
        q = qkva_ref[m * A_QBLOCK:(m + 1) * A_QBLOCK, j * LANES:(j + 1) * LANES]
        zero = jnp.zeros_like(q)
        qs = jnp.concatenate([jnp.where(low, q, zero), jnp.where(low, zero, q)], axis=0)
        kcols = slice(A_WIDTH + j * LANES, A_WIDTH + (j + 1) * LANES)
        dists = [d for d in range(A_KEY_TILES - 1, -1, -1) if m - d >= 0]
        key_rows = [slice((m - d) * A_QBLOCK, (m - d + 1) * A_QBLOCK) for d in dists]
        tile_max = []
        for t, (d, rows) in enumerate(zip(dists, key_rows)):
            s_ref[slot, t] = _dot_nt(qs, qkva_ref[rows, kcols])
            tile_max.append(jnp.max(s_ref[slot, t] + bias_a_ref[j, d], axis=-1, keepdims=True))
        return functools.reduce(jnp.maximum, tile_max), list(zip(dists, key_rows))

    def mixer_a_finish(slot, j, m, staged):
        mx, tiles = staged
        acc = None
        vcols = slice(2 * A_WIDTH + j * LANES, 2 * A_WIDTH + (j + 1) * LANES)
        for t, (d, rows) in enumerate(tiles):
            v_and_ones = jnp.concatenate([qkva_ref[rows, vcols], ones[:A_QBLOCK]], axis=1)
            p = jnp.exp2(s_ref[slot, t] + bias_a_ref[j, d] - mx).astype(BF16)
            pv = jnp.dot(p, v_and_ones, preferred_element_type=F32)
            acc = pv if acc is None else acc + pv
        o = acc[:, :LANES] / acc[:, LANES:]
        ya_ref[m * A_QBLOCK:(m + 1) * A_QBLOCK, j * LANES:(j + 1) * LANES] = jnp.where(
            low, o[:A_QBLOCK], o[A_QBLOCK:]).astype(BF16)

    def mixer_b_scores(slot, m):
        krows = slice(max(m - 1, 0) * B_QBLOCK, (m + 1) * B_QBLOCK)
        width = krows.stop - krows.start
        maxima = []
        for g in range(B_KV_HEADS):
            tiles = [qkvb_ref[m * B_QBLOCK:(m + 1) * B_QBLOCK, c * LANES:(c + 1) * LANES]
                     for c in range(g * B_GROUP // 2, (g + 1) * B_GROUP // 2)]
            zero = jnp.zeros_like(tiles[0])
            qs = jnp.concatenate([piece for t in tiles
                                  for piece in (jnp.where(low, t, zero), jnp.where(low, zero, t))],
                                 axis=0)
            logits = _dot_nt(qs, kdup_ref[g, krows, :]) + bias_b_ref[g, :, B_WINDOW - width:]
            s_ref[slot, g, :, :width] = logits
            maxima.append(jnp.maximum(jnp.max(logits, axis=-1, keepdims=True), sink[g]))
        return maxima

    def mixer_b_finish(slot, m, staged):
        qrows = slice(m * B_QBLOCK, (m + 1) * B_QBLOCK)
        krows = slice(max(m - 1, 0) * B_QBLOCK, (m + 1) * B_QBLOCK)
        width = krows.stop - krows.start
        for g, mx in enumerate(staged):
            p = jnp.exp2(s_ref[slot, g, :, :width] - jnp.concatenate([mx] * (width // LANES), axis=1))
            pv = jnp.dot(p.astype(BF16), vext_b_ref[g, krows, :], preferred_element_type=F32)
            o = pv[:, :LANES] / (pv[:, LANES:] + jnp.exp2(sink[g] - mx))
            for t in range(B_GROUP // 2):
                c = g * B_GROUP // 2 + t
                even, odd = (o[(2 * t + h) * B_QBLOCK:(2 * t + h + 1) * B_QBLOCK] for h in (0, 1))
                yb_ref[qrows, c * LANES:(c + 1) * LANES] = jnp.where(low, even, odd).astype(BF16)

    for g in range(B_KV_HEADS):
        for src, dst in ((qkvb_ref[:, B_Q_WIDTH:B_Q_WIDTH + LANES], kdup_ref.at[g]),
                         (qkvb_ref[:, B_Q_WIDTH + LANES:], vext_b_ref.at[g])):
            half = src[:, g * D_HEAD:(g + 1) * D_HEAD]
            dst[:, :LANES] = jnp.concatenate([half, half], axis=1)
        vext_b_ref[g, :, LANES:] = ones
    sink = [jnp.concatenate([jnp.full((B_QBLOCK, LANES), sinks_ref[g * B_GROUP + c], F32)
                             for c in range(B_GROUP)], axis=0) for g in range(B_KV_HEADS)]

    b_blocks_per_pair = seq // B_QBLOCK // n_pairs
    blocks = []
    for j in range(n_pairs):
        blocks += [(functools.partial(mixer_a_scores, j=j, m=m),
                    functools.partial(mixer_a_finish, j=j, m=m)) for m in range(seq // A_QBLOCK)]
        blocks += [(functools.partial(mixer_b_scores, m=m), functools.partial(mixer_b_finish, m=m))
                   for m in range(j * b_blocks_per_pair, (j + 1) * b_blocks_per_pair)]
    staged = blocks[0][0](slot=0)
    for n, (_, finish) in enumerate(blocks):
        next_staged = blocks[n + 1][0](slot=(n + 1) % 2) if n + 1 < len(blocks) else None
        finish(slot=n % 2, staged=staged)
        staged = next_staged


def _resident(shape):
    return pl.BlockSpec(shape, lambda *_: (0,) * len(shape), pipeline_mode=pl.Buffered(1))


def _bias_a_table_index():
    u = np.arange(2 * A_QBLOCK)
    key_minus_query = np.where(u <= A_QBLOCK, u, u - 2 * A_QBLOCK)
    rel = np.arange(A_KEY_TILES)[:, None] * A_QBLOCK - key_minus_query[None, :]
    return np.clip(rel, -(CHUNK - 1), MAX_REL) + (CHUNK - 1)


def _bias_b_table():
    slopes = np.array([2.0 ** (-8.0 * (h + 1) / B_Q_HEADS) for h in range(B_Q_HEADS)], np.float32)
    i = np.arange(B_QBLOCK)[:, None]
    j = np.arange(B_WINDOW)[None, :]
    rel = i - j + B_QBLOCK
    chunk_diff = i // CHUNK - j // CHUNK + B_QBLOCK // CHUNK
    vis = (chunk_diff >= 0) & (chunk_diff <= B_PREV_CHUNKS)
    dist = np.abs(rel).astype(np.float32)
    out = np.empty((B_KV_HEADS, B_GROUP * B_QBLOCK, B_WINDOW), np.float32)
    for g in range(B_KV_HEADS):
        for c in range(B_GROUP):
            out[g, c * B_QBLOCK:(c + 1) * B_QBLOCK] = np.where(
                vis, -slopes[g * B_GROUP + c] * dist * LOG2_E, NEG_INF)
    return out


def kernel(x, ffn1_norm, ffn1_w_gate, ffn1_w_up, ffn1_w_down, mix_norm, w_in, rel_bias, sinks,
           w_proj_a, w_proj_b, w_out, ffn2_norm, ffn2_w_gate, ffn2_w_up, ffn2_w_down, final_norm):
    batch, seq, d = x.shape
    rows = batch * seq
    assert d == D_MODEL and rows % ROW_TILE == 0 and seq % A_QBLOCK == 0
    n_row_tiles = rows // ROW_TILE
    vec = lambda g: g.reshape(1, D_MODEL).astype(F32)

    dense_params = pltpu.CompilerParams(dimension_semantics=("arbitrary",),
                                        vmem_limit_bytes=VMEM_LIMIT_BYTES)

    stage0_tile = lambda width, col=0: pl.BlockSpec(
        (ROW_TILE, width), lambda i: (jnp.minimum(i, n_row_tiles - 1), col))
    stage1_tile = lambda width: pl.BlockSpec(
        (ROW_TILE, width), lambda i: (jnp.maximum(i - 1, 0), 0))
    x1, qkva, qkvb, gates = pl.pallas_call(
        _ffn_in_kernel,
        grid=(n_row_tiles + 1,),
        in_specs=[stage0_tile(D_MODEL), _resident((1, D_MODEL)),
                  _resident((D_MODEL, D_FF)), _resident((D_MODEL, D_FF)),
                  _resident((D_FF, D_MODEL)), _resident((1, D_MODEL)),
                  _resident(w_in.shape)],
        out_specs=[stage0_tile(D_MODEL), stage1_tile(QKV_A_WIDTH), stage1_tile(QKV_B_WIDTH),
                   stage1_tile(GATES_WIDTH)],
        out_shape=[jax.ShapeDtypeStruct((rows, D_MODEL), F32),
                   jax.ShapeDtypeStruct((rows, QKV_A_WIDTH), BF16),
                   jax.ShapeDtypeStruct((rows, QKV_B_WIDTH), BF16),
                   jax.ShapeDtypeStruct((rows, GATES_WIDTH), BF16)],
        scratch_shapes=[pltpu.VMEM((ROW_TILE, D_FF), BF16), pltpu.VMEM((ROW_TILE, D_MODEL), BF16)],
        compiler_params=dense_params,
        name="ffn_in",
    )(x.reshape(rows, D_MODEL), vec(ffn1_norm), ffn1_w_gate.astype(BF16),
      ffn1_w_up.astype(BF16), ffn1_w_down.astype(BF16), vec(mix_norm), w_in.astype(BF16))

    tab_a = (rel_bias.astype(F32) * LOG2_E)[:, _bias_a_table_index()][:, :, None, :]
    n_pairs = A_WIDTH // LANES
    batch_block = lambda width: pl.BlockSpec((None, seq, width), lambda b: (b, 0, 0))
    ffn2_weights = (ffn2_w_gate, ffn2_w_up, ffn2_w_down)
    assert all(w.shape[0] % (batch * BF16_SUBLANES) == 0 for w in ffn2_weights)
    cast_block = lambda w: pl.BlockSpec((w.shape[0] // batch, w.shape[1]), lambda b: (b, 0))
    ya, yb, wg2, wu2, wd2 = pl.pallas_call(
        _attn_kernel,
        grid=(batch,),
        in_specs=[pl.BlockSpec(memory_space=pltpu.SMEM),
                  batch_block(QKV_A_WIDTH), batch_block(QKV_B_WIDTH),
                  _resident((A_HEADS, A_KEY_TILES, 1, 2 * A_QBLOCK)),
                  _resident((B_KV_HEADS, B_GROUP * B_QBLOCK, B_WINDOW))]
                 + [cast_block(w) for w in ffn2_weights],
        out_specs=[batch_block(A_WIDTH), batch_block(B_Q_WIDTH)]
                  + [cast_block(w) for w in ffn2_weights],
        out_shape=[jax.ShapeDtypeStruct((batch, seq, A_WIDTH), BF16),
                   jax.ShapeDtypeStruct((batch, seq, B_Q_WIDTH), BF16)]
                  + [jax.ShapeDtypeStruct(w.shape, BF16) for w in ffn2_weights],
        scratch_shapes=[pltpu.VMEM((n_pairs, A_KEY_TILES, 2 * A_QBLOCK, A_QBLOCK), F32),
                        pltpu.VMEM((B_KV_HEADS, seq, 2 * LANES), BF16),
                        pltpu.VMEM((B_KV_HEADS, seq, LANES), BF16),
                        pltpu.VMEM((2, A_KEY_TILES, 2 * A_QBLOCK, A_QBLOCK), F32)],
        compiler_params=pltpu.CompilerParams(dimension_semantics=("arbitrary",),
                                             vmem_limit_bytes=VMEM_LIMIT_BYTES),
        name="attn",
    )(sinks.astype(F32) * LOG2_E, qkva.reshape(batch, seq, QKV_A_WIDTH),
      qkvb.reshape(batch, seq, QKV_B_WIDTH), tab_a, jnp.asarray(_bias_b_table()), *ffn2_weights)

    out = pl.pallas_call(
        _ffn_out_kernel,
        grid=(n_row_tiles + 1,),
        in_specs=[stage0_tile(D_MODEL), stage0_tile(A_WIDTH), stage0_tile(B_Q_WIDTH),
                  stage0_tile(D_MODEL, col=0), stage0_tile(D_MODEL, col=1),
                  _resident((A_WIDTH, D_MODEL)), _resident((B_Q_WIDTH, D_MODEL)),
                  _resident((D_MODEL, D_MODEL)), _resident((1, D_MODEL)),
                  _resident((D_MODEL, D_FF)), _resident((D_MODEL, D_FF)),
                  _resident((D_FF, D_MODEL)), _resident((1, D_MODEL))],
        out_specs=stage1_tile(D_MODEL),
        out_shape=jax.ShapeDtypeStruct((rows, D_MODEL), F32),
        scratch_shapes=[pltpu.VMEM((ROW_TILE, D_FF), BF16), pltpu.VMEM((ROW_TILE, D_MODEL), F32),
                        pltpu.VMEM((ROW_TILE, D_MODEL), BF16), pltpu.VMEM((ROW_TILE, D_MODEL), BF16)],
        compiler_params=dense_params,
        name="ffn_out",
    )(x1, ya.reshape(rows, A_WIDTH), yb.reshape(rows, B_Q_WIDTH), gates, gates,
      w_proj_a.astype(BF16), w_proj_b.astype(BF16), w_out.astype(BF16), vec(ffn2_norm),
      wg2, wu2, wd2, vec(final_norm))
    return out.reshape(batch, seq, D_MODEL)
```

```python
import functools

import numpy as np
import jax
import jax.numpy as jnp
from jax import lax
from jax.experimental import pallas as pl
from jax.experimental.pallas import tpu as pltpu

F32 = jnp.float32
BF16 = jnp.bfloat16

D_MODEL = 1024
D_FF = 2816
CHUNK = 64
D_HEAD = 64
A_HEADS = 8
A_PREV_CHUNKS = 8
MAX_REL = 128
B_Q_HEADS = 8
B_KV_HEADS = 2
B_GROUP = B_Q_HEADS // B_KV_HEADS
B_PREV_CHUNKS = 2
A_WIDTH = A_HEADS * D_HEAD
B_Q_WIDTH = B_Q_HEADS * D_HEAD
B_KV_WIDTH = B_KV_HEADS * D_HEAD
QKV_A_WIDTH = 3 * A_WIDTH
QKV_B_WIDTH = B_Q_WIDTH + 2 * B_KV_WIDTH
GATES_WIDTH = 2 * D_MODEL
EPS = 1e-6
NEG_INF = -1e30
LOG2_E = float(np.log2(np.e))
QK_SCALE = LOG2_E / float(np.sqrt(D_HEAD))

LANES = 128
BF16_SUBLANES = 16
MXU_DIM = 256
VMEM_LIMIT_BYTES = 56 * 1024 * 1024

ROW_TILE = 512
FF_CHUNK = MXU_DIM
A_QBLOCK = 4 * CHUNK
A_KEY_TILES = A_PREV_CHUNKS * CHUNK // A_QBLOCK + 1
B_QBLOCK = 2 * CHUNK
B_WINDOW = 2 * B_QBLOCK
assert B_GROUP * B_QBLOCK == 2 * A_QBLOCK and B_WINDOW == A_QBLOCK and B_KV_HEADS <= A_KEY_TILES


def _rms_norm(x, gain):
    return x * lax.rsqrt(jnp.mean(x * x, axis=-1, keepdims=True) + EPS) * gain


def _swiglu(h, wg_ref, wu_ref, wd_ref, hid_ref, between_chunks=None):
    between_chunks = between_chunks or {}
    for c in range(D_FF // FF_CHUNK):
        cols = slice(c * FF_CHUNK, (c + 1) * FF_CHUNK)
        g = jnp.dot(h, wg_ref[:, cols], preferred_element_type=F32)
        u = jnp.dot(h, wu_ref[:, cols], preferred_element_type=F32)
        hid_ref[:, cols] = (g * jax.nn.sigmoid(g) * u).astype(BF16)
        if c in between_chunks:
            between_chunks[c]()
    return jnp.dot(hid_ref[...], wd_ref[...], preferred_element_type=F32)


def _ffn_in_kernel(x_ref, g1_ref, wg_ref, wu_ref, wd_ref, gm_ref, win_ref,
                   x1_ref, qkva_ref, qkvb_ref, gates_ref, hid_ref, h2_ref):
    i = pl.program_id(0)
    last = pl.num_programs(0) - 1

    def ffn_stage():
        x = x_ref[...]
        h = _rms_norm(x, g1_ref[...]).astype(BF16)
        x1 = x + 0.5 * _swiglu(h, wg_ref, wu_ref, wd_ref, hid_ref)
        x1_ref[...] = x1
        h2_ref[...] = _rms_norm(x1, gm_ref[...]).astype(BF16)

    def projection_stage():
        h2 = h2_ref[...]
        col = 0
        for o_ref, q_width in ((qkva_ref, A_WIDTH), (qkvb_ref, B_Q_WIDTH), (gates_ref, 0)):
            for lo, hi, factor in ((0, q_width, QK_SCALE), (q_width, o_ref.shape[1], None)):
                if hi > lo:
                    y = jnp.dot(h2, win_ref[:, col + lo:col + hi], preferred_element_type=F32)
                    o_ref[:, lo:hi] = (y if factor is None else y * factor).astype(BF16)
            col += o_ref.shape[1]

    @pl.when(i == 0)
    def _first():
        ffn_stage()

    @pl.when((i > 0) & (i < last))
    def _steady():
        projection_stage()
        ffn_stage()

    @pl.when(i == last)
    def _last():
        projection_stage()


def _ffn_out_kernel(x1_ref, ya_ref, yb_ref, ga_ref, gb_ref, wpa_ref, wpb_ref, wo_ref,
                    g2_ref, wg_ref, wu_ref, wd_ref, gf_ref, out_ref, hid_ref, x2_ref, h_ref,
                    mg_ref):
    i = pl.program_id(0)
    last = pl.num_programs(0) - 1
    merged_tile = {}

    def gate_columns(c):
        cols = slice(c * MXU_DIM, (c + 1) * MXU_DIM)
        pa = jnp.dot(ya_ref[...], wpa_ref[:, cols], preferred_element_type=F32)
        pb = jnp.dot(yb_ref[...], wpb_ref[:, cols], preferred_element_type=F32)
        mg_ref[:, cols] = (jax.nn.sigmoid(ga_ref[:, cols].astype(F32)) * pa
                           + jax.nn.sigmoid(gb_ref[:, cols].astype(F32)) * pb).astype(BF16)

    def mix_in():
        x2 = x1_ref[...] + jnp.dot(mg_ref[...], wo_ref[...], preferred_element_type=F32)
        merged_tile["x2"] = x2
        merged_tile["h"] = _rms_norm(x2, g2_ref[...]).astype(BF16)

    def store_merged_tile():
        x2_ref[...] = merged_tile["x2"]
        h_ref[...] = merged_tile["h"]

    merge_pieces = [functools.partial(gate_columns, c) for c in range(D_MODEL // MXU_DIM)]
    merge_pieces.append(mix_in)

    def ffn_stage(between_chunks=None):
        x3 = x2_ref[...] + 0.5 * _swiglu(h_ref[...], wg_ref, wu_ref, wd_ref, hid_ref,
                                         between_chunks)
        out_ref[...] = _rms_norm(x3, gf_ref[...])

    @pl.when(i == 0)
    def _first():
        for piece in merge_pieces:
            piece()
        store_merged_tile()

    @pl.when((i > 0) & (i < last))
    def _steady():
        ffn_stage(dict(zip((0, 1, 2, 3, 5), merge_pieces)))
        store_merged_tile()

    @pl.when(i == last)
    def _last():
        ffn_stage()


def _dot_nt(a, b):
    return lax.dot_general(a, b, (((1,), (1,)), ((), ())), preferred_element_type=F32)


def _attn_kernel(sinks_ref, qkva_ref, qkvb_ref, tab_ref, bias_b_ref, wg2_ref, wu2_ref, wd2_ref,
                 ya_ref, yb_ref, wg2_bf16_ref, wu2_bf16_ref, wd2_bf16_ref,
                 bias_a_ref, s_ref):
    seq = qkva_ref.shape[0]
    n_pairs = A_WIDTH // LANES
    low = lax.broadcasted_iota(jnp.int32, (1, LANES), 1) < D_HEAD
    ones = jnp.ones((seq, LANES), BF16)
    for src, dst in ((wg2_ref, wg2_bf16_ref), (wu2_ref, wu2_bf16_ref), (wd2_ref, wd2_bf16_ref)):
        dst[...] = src[...].astype(BF16)

    @pl.when(pl.program_id(0) == 0)
    def _build_bias_a():
        qc = lax.broadcasted_iota(jnp.int32, (A_QBLOCK, A_QBLOCK), 0) // CHUNK
        kc = lax.broadcasted_iota(jnp.int32, (A_QBLOCK, A_QBLOCK), 1) // CHUNK
        for h in range(A_HEADS):
            for d in range(A_KEY_TILES):
                row = jnp.broadcast_to(tab_ref[h, d], (A_QBLOCK, 2 * A_QBLOCK))
                toeplitz = pltpu.roll(row, 0, 1, stride=1, stride_axis=0)[:, :A_QBLOCK]
                chunk_diff = d * (A_QBLOCK // CHUNK) + qc - kc
                visible = (chunk_diff >= 0) & (chunk_diff <= A_PREV_CHUNKS)
                bias_a_ref[h // 2, d, (h % 2) * A_QBLOCK:(h % 2 + 1) * A_QBLOCK, :] = jnp.where(
                    visible, toeplitz, NEG_INF)

    def mixer_a_scores(slot, j, m):
        q = qkva_ref[m * A_QBLOCK:(m + 1) * A_QBLOCK, j * LANES:(j + 1) * LANES]
        zero = jnp.zeros_like(q)
        qs = jnp.concatenate([jnp.where(low, q, zero), jnp.where(low, zero, q)], axis=0)
        kcols = slice(A_WIDTH + j * LANES, A_WIDTH + (j + 1) * LANES)
        dists = [d for d in range(A_KEY_TILES - 1, -1, -1) if m - d >= 0]
        key_rows = [slice((m - d) * A_QBLOCK, (m - d + 1) * A_QBLOCK) for d in dists]
        tile_max = []
        for t, (d, rows) in enumerate(zip(dists, key_rows)):
            logits = _dot_nt(qs, qkva_ref[rows, kcols]) + bias_a_ref[j, d]
            s_ref[slot, t] = logits
            tile_max.append(jnp.max(logits, axis=-1, keepdims=True))
        return functools.reduce(jnp.maximum, tile_max), key_rows

    def mixer_a_finish(slot, j, m, staged):
        mx, key_rows = staged
        acc = None
        vcols = slice(2 * A_WIDTH + j * LANES, 2 * A_WIDTH + (j + 1) * LANES)
        for t, rows in enumerate(key_rows):
            v_and_ones = jnp.concatenate([qkva_ref[rows, vcols], ones[:A_QBLOCK]], axis=1)
            pv = jnp.dot(jnp.exp2(s_ref[slot, t] - mx).astype(BF16), v_and_ones,
                         preferred_element_type=F32)
            acc = pv if acc is None else acc + pv
        o = acc[:, :LANES] / acc[:, LANES:]
        ya_ref[m * A_QBLOCK:(m + 1) * A_QBLOCK, j * LANES:(j + 1) * LANES] = jnp.where(
            low, o[:A_QBLOCK], o[A_QBLOCK:]).astype(BF16)

    def mixer_b_scores(slot, m):
        krows = slice(max(m - 1, 0) * B_QBLOCK, (m + 1) * B_QBLOCK)
        width = krows.stop - krows.start
        maxima = []
        for g in range(B_KV_HEADS):
            tiles = [qkvb_ref[m * B_QBLOCK:(m + 1) * B_QBLOCK, c * LANES:(c + 1) * LANES]
                     for c in range(g * B_GROUP // 2, (g + 1) * B_GROUP // 2)]
            zero = jnp.zeros_like(tiles[0])
            qs = jnp.concatenate([piece for t in tiles
                                  for piece in (jnp.where(low, t, zero), jnp.where(low, zero, t))],
                                 axis=0)
            k_half = qkvb_ref[krows, B_Q_WIDTH + g * D_HEAD:B_Q_WIDTH + (g + 1) * D_HEAD]
            logits = (_dot_nt(qs, jnp.concatenate([k_half, k_half], axis=1))
                      + bias_b_ref[g, :, B_WINDOW - width:])
            s_ref[slot, g, :, :width] = logits
            maxima.append(jnp.maximum(jnp.max(logits, axis=-1, keepdims=True), sink[g]))
        return maxima

    def mixer_b_finish(slot, m, staged):
        qrows = slice(m * B_QBLOCK, (m + 1) * B_QBLOCK)
        krows = slice(max(m - 1, 0) * B_QBLOCK, (m + 1) * B_QBLOCK)
        width = krows.stop - krows.start
        for g, mx in enumerate(staged):
            p = jnp.exp2(s_ref[slot, g, :, :width] - jnp.concatenate([mx] * (width // LANES), axis=1))
            v_half = qkvb_ref[krows, B_Q_WIDTH + LANES + g * D_HEAD:
                              B_Q_WIDTH + LANES + (g + 1) * D_HEAD]
            v_and_ones = jnp.concatenate([v_half, v_half, ones[:width]], axis=1)
            pv = jnp.dot(p.astype(BF16), v_and_ones, preferred_element_type=F32)
            o = pv[:, :LANES] / (pv[:, LANES:] + jnp.exp2(sink[g] - mx))
            for t in range(B_GROUP // 2):
                c = g * B_GROUP // 2 + t
                even, odd = (o[(2 * t + h) * B_QBLOCK:(2 * t + h + 1) * B_QBLOCK] for h in (0, 1))
                yb_ref[qrows, c * LANES:(c + 1) * LANES] = jnp.where(low, even, odd).astype(BF16)

    sink = [jnp.concatenate([jnp.full((B_QBLOCK, LANES), sinks_ref[g * B_GROUP + c], F32)
                             for c in range(B_GROUP)], axis=0) for g in range(B_KV_HEADS)]

    b_blocks_per_pair = seq // B_QBLOCK // n_pairs
    blocks = []
    for j in range(n_pairs):
        blocks += [(functools.partial(mixer_a_scores, j=j, m=m),
                    functools.partial(mixer_a_finish, j=j, m=m)) for m in range(seq // A_QBLOCK)]
        blocks += [(functools.partial(mixer_b_scores, m=m), functools.partial(mixer_b_finish, m=m))
                   for m in range(j * b_blocks_per_pair, (j + 1) * b_blocks_per_pair)]
    staged = blocks[0][0](slot=0)
    for n, (_, finish) in enumerate(blocks):
        next_staged = blocks[n + 1][0](slot=(n + 1) % 2) if n + 1 < len(blocks) else None
        finish(slot=n % 2, staged=staged)
        staged = next_staged


def _resident(shape):
    return pl.BlockSpec(shape, lambda *_: (0,) * len(shape), pipeline_mode=pl.Buffered(1))


def _bias_a_table_index():
    u = np.arange(2 * A_QBLOCK)
    key_minus_query = np.where(u <= A_QBLOCK, u, u - 2 * A_QBLOCK)
    rel = np.arange(A_KEY_TILES)[:, None] * A_QBLOCK - key_minus_query[None, :]
    return np.clip(rel, -(CHUNK - 1), MAX_REL) + (CHUNK - 1)


def _bias_b_table():
    slopes = np.array([2.0 ** (-8.0 * (h + 1) / B_Q_HEADS) for h in range(B_Q_HEADS)], np.float32)
    i = np.arange(B_QBLOCK)[:, None]
    j = np.arange(B_WINDOW)[None, :]
    rel = i - j + B_QBLOCK
    chunk_diff = i // CHUNK - j // CHUNK + B_QBLOCK // CHUNK
    vis = (chunk_diff >= 0) & (chunk_diff <= B_PREV_CHUNKS)
    dist = np.abs(rel).astype(np.float32)
    out = np.empty((B_KV_HEADS, B_GROUP * B_QBLOCK, B_WINDOW), np.float32)
    for g in range(B_KV_HEADS):
        for c in range(B_GROUP):
            out[g, c * B_QBLOCK:(c + 1) * B_QBLOCK] = np.where(
                vis, -slopes[g * B_GROUP + c] * dist * LOG2_E, NEG_INF)
    return out


def kernel(x, ffn1_norm, ffn1_w_gate, ffn1_w_up, ffn1_w_down, mix_norm, w_in, rel_bias, sinks,
           w_proj_a, w_proj_b, w_out, ffn2_norm, ffn2_w_gate, ffn2_w_up, ffn2_w_down, final_norm):
    batch, seq, d = x.shape
    rows = batch * seq
    assert d == D_MODEL and rows % ROW_TILE == 0 and seq % A_QBLOCK == 0
    n_row_tiles = rows // ROW_TILE
    vec = lambda g: g.reshape(1, D_MODEL).astype(F32)

    dense_params = pltpu.CompilerParams(dimension_semantics=("arbitrary",),
                                        vmem_limit_bytes=VMEM_LIMIT_BYTES)

    stage0_tile = lambda width, col=0: pl.BlockSpec(
        (ROW_TILE, width), lambda i: (jnp.minimum(i, n_row_tiles - 1), col))
    stage1_tile = lambda width: pl.BlockSpec(
        (ROW_TILE, width), lambda i: (jnp.maximum(i - 1, 0), 0))
    x1, qkva, qkvb, gates = pl.pallas_call(
        _ffn_in_kernel,
        grid=(n_row_tiles + 1,),
        in_specs=[stage0_tile(D_MODEL), _resident((1, D_MODEL)),
                  _resident((D_MODEL, D_FF)), _resident((D_MODEL, D_FF)),
                  _resident((D_FF, D_MODEL)), _resident((1, D_MODEL)),
                  _resident(w_in.shape)],
        out_specs=[stage0_tile(D_MODEL), stage1_tile(QKV_A_WIDTH), stage1_tile(QKV_B_WIDTH),
                   stage1_tile(GATES_WIDTH)],
        out_shape=[jax.ShapeDtypeStruct((rows, D_MODEL), F32),
                   jax.ShapeDtypeStruct((rows, QKV_A_WIDTH), BF16),
                   jax.ShapeDtypeStruct((rows, QKV_B_WIDTH), BF16),
                   jax.ShapeDtypeStruct((rows, GATES_WIDTH), BF16)],
        scratch_shapes=[pltpu.VMEM((ROW_TILE, D_FF), BF16), pltpu.VMEM((ROW_TILE, D_MODEL), BF16)],
        compiler_params=dense_params,
        name="ffn_in",
    )(x.reshape(rows, D_MODEL), vec(ffn1_norm), ffn1_w_gate.astype(BF16),
      ffn1_w_up.astype(BF16), ffn1_w_down.astype(BF16), vec(mix_norm), w_in.astype(BF16))

    tab_a = (rel_bias.astype(F32) * LOG2_E)[:, _bias_a_table_index()][:, :, None, :]
    n_pairs = A_WIDTH // LANES
    batch_block = lambda width: pl.BlockSpec((None, seq, width), lambda b: (b, 0, 0))
    ffn2_weights = (ffn2_w_gate, ffn2_w_up, ffn2_w_down)
    assert all(w.shape[0] % (batch * BF16_SUBLANES) == 0 for w in ffn2_weights)
    cast_block = lambda w: pl.BlockSpec((w.shape[0] // batch, w.shape[1]), lambda b: (b, 0))
    ya, yb, wg2, wu2, wd2 = pl.pallas_call(
        _attn_kernel,
        grid=(batch,),
        in_specs=[pl.BlockSpec(memory_space=pltpu.SMEM),
                  batch_block(QKV_A_WIDTH), batch_block(QKV_B_WIDTH),
                  _resident((A_HEADS, A_KEY_TILES, 1, 2 * A_QBLOCK)),
                  _resident((B_KV_HEADS, B_GROUP * B_QBLOCK, B_WINDOW))]
                 + [cast_block(w) for w in ffn2_weights],
        out_specs=[batch_block(A_WIDTH), batch_block(B_Q_WIDTH)]
                  + [cast_block(w) for w in ffn2_weights],
        out_shape=[jax.ShapeDtypeStruct((batch, seq, A_WIDTH), BF16),
                   jax.ShapeDtypeStruct((batch, seq, B_Q_WIDTH), BF16)]
                  + [jax.ShapeDtypeStruct(w.shape, BF16) for w in ffn2_weights],
        scratch_shapes=[pltpu.VMEM((n_pairs, A_KEY_TILES, 2 * A_QBLOCK, A_QBLOCK), F32),
                        pltpu.VMEM((2, A_KEY_TILES, 2 * A_QBLOCK, A_QBLOCK), F32)],
        compiler_params=pltpu.CompilerParams(dimension_semantics=("arbitrary",),
                                             vmem_limit_bytes=VMEM_LIMIT_BYTES),
        name="attn",
    )(sinks.astype(F32) * LOG2_E, qkva.reshape(batch, seq, QKV_A_WIDTH),
      qkvb.reshape(batch, seq, QKV_B_WIDTH), tab_a, jnp.asarray(_bias_b_table()), *ffn2_weights)

    out = pl.pallas_call(
        _ffn_out_kernel,
        grid=(n_row_tiles + 1,),
        in_specs=[stage0_tile(D_MODEL), stage0_tile(A_WIDTH), stage0_tile(B_Q_WIDTH),
                  stage0_tile(D_MODEL, col=0), stage0_tile(D_MODEL, col=1),
                  _resident((A_WIDTH, D_MODEL)), _resident((B_Q_WIDTH, D_MODEL)),
                  _resident((D_MODEL, D_MODEL)), _resident((1, D_MODEL)),
                  _resident((D_MODEL, D_FF)), _resident((D_MODEL, D_FF)),
                  _resident((D_FF, D_MODEL)), _resident((1, D_MODEL))],
        out_specs=stage1_tile(D_MODEL),
        out_shape=jax.ShapeDtypeStruct((rows, D_MODEL), F32),
        scratch_shapes=[pltpu.VMEM((ROW_TILE, D_FF), BF16), pltpu.VMEM((ROW_TILE, D_MODEL), F32),
                        pltpu.VMEM((ROW_TILE, D_MODEL), BF16), pltpu.VMEM((ROW_TILE, D_MODEL), BF16)],
        compiler_params=dense_params,
        name="ffn_out",
    )(x1, ya.reshape(rows, A_WIDTH), yb.reshape(rows, B_Q_WIDTH), gates, gates,
      w_proj_a.astype(BF16), w_proj_b.astype(BF16), w_out.astype(BF16), vec(ffn2_norm),
      wg2, wu2, wd2, vec(final_norm))
    return out.reshape(batch, seq, D_MODEL)
```

```python
import functools

import numpy as np
import jax
import jax.numpy as jnp
from jax import lax
from jax.experimental import pallas as pl
from jax.experimental.pallas import tpu as pltpu

F32 = jnp.float32
BF16 = jnp.bfloat16

D_MODEL = 1024
D_FF = 2816
CHUNK = 64
D_HEAD = 64
A_HEADS = 8
A_PREV_CHUNKS = 8
MAX_REL = 128
B_Q_HEADS = 8
B_KV_HEADS = 2
B_GROUP = B_Q_HEADS // B_KV_HEADS
B_PREV_CHUNKS = 2
A_WIDTH = A_HEADS * D_HEAD
B_Q_WIDTH = B_Q_HEADS * D_HEAD
B_KV_WIDTH = B_KV_HEADS * D_HEAD
QKV_A_WIDTH = 3 * A_WIDTH
QKV_B_WIDTH = B_Q_WIDTH + 2 * B_KV_WIDTH
GATES_WIDTH = 2 * D_MODEL
EPS = 1e-6
NEG_INF = -1e30
LOG2_E = float(np.log2(np.e))
QK_SCALE = LOG2_E / float(np.sqrt(D_HEAD))

LANES = 128
BF16_SUBLANES = 16
MXU_DIM = 256
VMEM_LIMIT_BYTES = 56 * 1024 * 1024

ROW_TILE = 512
FF_CHUNK = MXU_DIM
A_QBLOCK = 4 * CHUNK
A_KEY_TILES = A_PREV_CHUNKS * CHUNK // A_QBLOCK + 1
B_QBLOCK = 2 * CHUNK
B_WINDOW = 2 * B_QBLOCK
assert B_GROUP * B_QBLOCK == 2 * A_QBLOCK and B_WINDOW == A_QBLOCK and B_KV_HEADS <= A_KEY_TILES


def _rms_norm(x, gain):
    return x * lax.rsqrt(jnp.mean(x * x, axis=-1, keepdims=True) + EPS) * gain


def _swiglu(h, wg_ref, wu_ref, wd_ref, hid_ref, between_chunks=None):
    between_chunks = between_chunks or {}
    for c in range(D_FF // FF_CHUNK):
        cols = slice(c * FF_CHUNK, (c + 1) * FF_CHUNK)
        g = jnp.dot(h, wg_ref[:, cols], preferred_element_type=F32)
        u = jnp.dot(h, wu_ref[:, cols], preferred_element_type=F32)
        hid_ref[:, cols] = (g * jax.nn.sigmoid(g) * u).astype(BF16)
        if c in between_chunks:
            between_chunks[c]()
    return jnp.dot(hid_ref[...], wd_ref[...], preferred_element_type=F32)


def _ffn_in_kernel(x_ref, g1_ref, wg_ref, wu_ref, wd_ref, gm_ref, win_ref,
                   x1_ref, qkva_ref, qkvb_ref, gates_ref, hid_ref, h2_ref):
    i = pl.program_id(0)
    last = pl.num_programs(0) - 1

    def ffn_stage():
        h = _rms_norm(x_ref[...], g1_ref[...]).astype(BF16)
        y = _swiglu(h, wg_ref, wu_ref, wd_ref, hid_ref)
        x1 = x_ref[...] + 0.5 * y
        x1_ref[...] = x1
        h2_ref[...] = _rms_norm(x1, gm_ref[...]).astype(BF16)

    def projection_stage():
        h2 = h2_ref[...]
        col = 0
        for o_ref, q_width in ((qkva_ref, A_WIDTH), (qkvb_ref, B_Q_WIDTH), (gates_ref, 0)):
            for lo, hi, factor in ((0, q_width, QK_SCALE), (q_width, o_ref.shape[1], None)):
                if hi > lo:
                    y = jnp.dot(h2, win_ref[:, col + lo:col + hi], preferred_element_type=F32)
                    o_ref[:, lo:hi] = (y if factor is None else y * factor).astype(BF16)
            col += o_ref.shape[1]

    @pl.when(i == 0)
    def _first():
        ffn_stage()

    @pl.when((i > 0) & (i < last))
    def _steady():
        projection_stage()
        ffn_stage()

    @pl.when(i == last)
    def _last():
        projection_stage()


def _ffn_out_kernel(x1_ref, ya_ref, yb_ref, ga_ref, gb_ref, wpa_ref, wpb_ref, wo_ref,
                    g2_ref, wg_ref, wu_ref, wd_ref, gf_ref, out_ref, hid_ref, x2_ref, h_ref,
                    mg_ref):
    i = pl.program_id(0)
    last = pl.num_programs(0) - 1
    merged_tile = {}

    def gate_columns(c):
        cols = slice(c * MXU_DIM, (c + 1) * MXU_DIM)
        pa = jnp.dot(ya_ref[...], wpa_ref[:, cols], preferred_element_type=F32)
        pb = jnp.dot(yb_ref[...], wpb_ref[:, cols], preferred_element_type=F32)
        mg_ref[:, cols] = (jax.nn.sigmoid(ga_ref[:, cols].astype(F32)) * pa
                           + jax.nn.sigmoid(gb_ref[:, cols].astype(F32)) * pb).astype(BF16)

    def mix_in():
        mixed = jnp.dot(mg_ref[...], wo_ref[...], preferred_element_type=F32)
        x2 = x1_ref[...] + mixed
        merged_tile["x2"] = x2
        merged_tile["h"] = _rms_norm(x2, g2_ref[...]).astype(BF16)

    def store_merged_tile():
        x2_ref[...] = merged_tile["x2"]
        h_ref[...] = merged_tile["h"]

    merge_pieces = [functools.partial(gate_columns, c) for c in range(D_MODEL // MXU_DIM)]
    merge_pieces.append(mix_in)

    def ffn_stage(between_chunks=None):
        y = _swiglu(h_ref[...], wg_ref, wu_ref, wd_ref, hid_ref, between_chunks)
        x3 = x2_ref[...] + 0.5 * y
        out_ref[...] = _rms_norm(x3, gf_ref[...])

    @pl.when(i == 0)
    def _first():
        for piece in merge_pieces:
            piece()
        store_merged_tile()

    @pl.when((i > 0) & (i < last))
    def _steady():
        ffn_stage(dict(zip((0, 1, 2, 3, 5), merge_pieces)))
        store_merged_tile()

    @pl.when(i == last)
    def _last():
        ffn_stage()


def _dot_nt(a, b):
    return lax.dot_general(a, b, (((1,), (1,)), ((), ())), preferred_element_type=F32)


def _attn_kernel(sinks_ref, qkva_ref, qkvb_ref, tab_ref, bias_b_ref, wg2_ref, wu2_ref, wd2_ref,
                 ya_ref, yb_ref, wg2_bf16_ref, wu2_bf16_ref, wd2_bf16_ref,
                 bias_a_ref, vext_b_ref, kdup_ref, s_ref):
    seq = qkva_ref.shape[0]
    n_pairs = A_WIDTH // LANES
    low = lax.broadcasted_iota(jnp.int32, (1, LANES), 1) < D_HEAD
    ones = jnp.ones((seq, LANES), BF16)
    for src, dst in ((wg2_ref, wg2_bf16_ref), (wu2_ref, wu2_bf16_ref), (wd2_ref, wd2_bf16_ref)):
        dst[...] = src[...].astype(BF16)

    @pl.when(pl.program_id(0) == 0)
    def _build_bias_a():
        qc = lax.broadcasted_iota(jnp.int32, (A_QBLOCK, A_QBLOCK), 0) // CHUNK
        kc = lax.broadcasted_iota(jnp.int32, (A_QBLOCK, A_QBLOCK), 1) // CHUNK
        for h in range(A_HEADS):
            for d in range(A_KEY_TILES):
                row = jnp.broadcast_to(tab_ref[h, d], (A_QBLOCK, 2 * A_QBLOCK))
                toeplitz = pltpu.roll(row, 0, 1, stride=1, stride_axis=0)[:, :A_QBLOCK]
                chunk_diff = d * (A_QBLOCK // CHUNK) + qc - kc
                visible = (chunk_diff >= 0) & (chunk_diff <= A_PREV_CHUNKS)
                bias_a_ref[h // 2, d, (h % 2) * A_QBLOCK:(h % 2 + 1) * A_QBLOCK, :] = jnp.where(
                    visible, toeplitz, NEG_INF)

    def mixer_a_scores(slot, j, m):
        q = qkva_ref[m * A_QBLOCK:(m + 1) * A_QBLOCK, j * LANES:(j + 1) * LANES]
        zero = jnp.zeros_like(q)
        qs = jnp.concatenate([jnp.where(low, q, zero), jnp.where(low, zero, q)], axis=0)
        kcols = slice(A_WIDTH + j * LANES, A_WIDTH + (j + 1) * LANES)
        dists = [d for d in range(A_KEY_TILES - 1, -1, -1) if m - d >= 0]
        key_rows = [slice((m - d) * A_QBLOCK, (m - d + 1) * A_QBLOCK) for d in dists]
        tile_max = []
        for t, (d, rows) in enumerate(zip(dists, key_rows)):
            logits = _dot_nt(qs, qkva_ref[rows, kcols]) + bias_a_ref[j, d]
            s_ref[slot, t] = logits
            tile_max.append(jnp.max(logits, axis=-1, keepdims=True))
        return functools.reduce(jnp.maximum, tile_max), key_rows

    def mixer_a_finish(slot, j, m, staged):
        mx, key_rows = staged
        acc = None
        vcols = slice(2 * A_WIDTH + j * LANES, 2 * A_WIDTH + (j + 1) * LANES)
        for t, rows in enumerate(key_rows):
            v_and_ones = jnp.concatenate([qkva_ref[rows, vcols], ones[:A_QBLOCK]], axis=1)
            pv = jnp.dot(jnp.exp2(s_ref[slot, t] - mx).astype(BF16), v_and_ones,
                         preferred_element_type=F32)
            acc = pv if acc is None else acc + pv
        o = acc[:, :LANES] / acc[:, LANES:]
        ya_ref[m * A_QBLOCK:(m + 1) * A_QBLOCK, j * LANES:(j + 1) * LANES] = jnp.where(
            low, o[:A_QBLOCK], o[A_QBLOCK:]).astype(BF16)

    def mixer_b_scores(slot, m):
        krows = slice(max(m - 1, 0) * B_QBLOCK, (m + 1) * B_QBLOCK)
        width = krows.stop - krows.start
        maxima = []
        for g in range(B_KV_HEADS):
            tiles = [qkvb_ref[m * B_QBLOCK:(m + 1) * B_QBLOCK, c * LANES:(c + 1) * LANES]
                     for c in range(g * B_GROUP // 2, (g + 1) * B_GROUP // 2)]
            zero = jnp.zeros_like(tiles[0])
            qs = jnp.concatenate([piece for t in tiles
                                  for piece in (jnp.where(low, t, zero), jnp.where(low, zero, t))],
                                 axis=0)
            logits = _dot_nt(qs, kdup_ref[g, krows, :]) + bias_b_ref[g, :, B_WINDOW - width:]
            s_ref[slot, g, :, :width] = logits
            maxima.append(jnp.maximum(jnp.max(logits, axis=-1, keepdims=True), sink[g]))
        return maxima

    def mixer_b_finish(slot, m, staged):
        qrows = slice(m * B_QBLOCK, (m + 1) * B_QBLOCK)
        krows = slice(max(m - 1, 0) * B_QBLOCK, (m + 1) * B_QBLOCK)
        width = krows.stop - krows.start
        for g, mx in enumerate(staged):
            p = jnp.exp2(s_ref[slot, g, :, :width] - jnp.concatenate([mx] * (width // LANES), axis=1))
            pv = jnp.dot(p.astype(BF16), vext_b_ref[g, krows, :], preferred_element_type=F32)
            o = pv[:, :LANES] / (pv[:, LANES:] + jnp.exp2(sink[g] - mx))
            for t in range(B_GROUP // 2):
                c = g * B_GROUP // 2 + t
                even, odd = (o[(2 * t + h) * B_QBLOCK:(2 * t + h + 1) * B_QBLOCK] for h in (0, 1))
                yb_ref[qrows, c * LANES:(c + 1) * LANES] = jnp.where(low, even, odd).astype(BF16)

    for g in range(B_KV_HEADS):
        for src, dst in ((qkvb_ref[:, B_Q_WIDTH:B_Q_WIDTH + LANES], kdup_ref.at[g]),
                         (qkvb_ref[:, B_Q_WIDTH + LANES:], vext_b_ref.at[g])):
            half = src[:, g * D_HEAD:(g + 1) * D_HEAD]
            dst[:, :LANES] = jnp.concatenate([half, half], axis=1)
        vext_b_ref[g, :, LANES:] = ones
    sink = [jnp.concatenate([jnp.full((B_QBLOCK, LANES), sinks_ref[g * B_GROUP + c], F32)
                             for c in range(B_GROUP)], axis=0) for g in range(B_KV_HEADS)]

    b_blocks_per_pair = seq // B_QBLOCK // n_pairs
    blocks = []
    for j in range(n_pairs):
        blocks += [(functools.partial(mixer_a_scores, j=j, m=m),
                    functools.partial(mixer_a_finish, j=j, m=m)) for m in range(seq // A_QBLOCK)]
        blocks += [(functools.partial(mixer_b_scores, m=m), functools.partial(mixer_b_finish, m=m))
                   for m in range(j * b_blocks_per_pair, (j + 1) * b_blocks_per_pair)]
    staged = blocks[0][0](slot=0)
    for n, (_, finish) in enumerate(blocks):
        next_staged = blocks[n + 1][0](slot=(n + 1) % 2) if n + 1 < len(blocks) else None
        finish(slot=n % 2, staged=staged)
        staged = next_staged


def _resident(shape):
    return pl.BlockSpec(shape, lambda *_: (0,) * len(shape), pipeline_mode=pl.Buffered(1))


def _bias_a_table_index():
    u = np.arange(2 * A_QBLOCK)
    key_minus_query = np.where(u <= A_QBLOCK, u, u - 2 * A_QBLOCK)
    rel = np.arange(A_KEY_TILES)[:, None] * A_QBLOCK - key_minus_query[None, :]
    return np.clip(rel, -(CHUNK - 1), MAX_REL) + (CHUNK - 1)


def _bias_b_table():
    slopes = np.array([2.0 ** (-8.0 * (h + 1) / B_Q_HEADS) for h in range(B_Q_HEADS)], np.float32)
    i = np.arange(B_QBLOCK)[:, None]
    j = np.arange(B_WINDOW)[None, :]
    rel = i - j + B_QBLOCK
    chunk_diff = i // CHUNK - j // CHUNK + B_QBLOCK // CHUNK
    vis = (chunk_diff >= 0) & (chunk_diff <= B_PREV_CHUNKS)
    dist = np.abs(rel).astype(np.float32)
    out = np.empty((B_KV_HEADS, B_GROUP * B_QBLOCK, B_WINDOW), np.float32)
    for g in range(B_KV_HEADS):
        for c in range(B_GROUP):
            out[g, c * B_QBLOCK:(c + 1) * B_QBLOCK] = np.where(
                vis, -slopes[g * B_GROUP + c] * dist * LOG2_E, NEG_INF)
    return out


def kernel(x, ffn1_norm, ffn1_w_gate, ffn1_w_up, ffn1_w_down, mix_norm, w_in, rel_bias, sinks,
           w_proj_a, w_proj_b, w_out, ffn2_norm, ffn2_w_gate, ffn2_w_up, ffn2_w_down, final_norm):
    batch, seq, d = x.shape
    rows = batch * seq
    assert d == D_MODEL and rows % ROW_TILE == 0 and seq % A_QBLOCK == 0
    n_row_tiles = rows // ROW_TILE
    vec = lambda g: g.reshape(1, D_MODEL).astype(F32)

    dense_params = pltpu.CompilerParams(dimension_semantics=("arbitrary",),
                                        vmem_limit_bytes=VMEM_LIMIT_BYTES)

    stage0_tile = lambda width, col=0: pl.BlockSpec(
        (ROW_TILE, width), lambda i: (jnp.minimum(i, n_row_tiles - 1), col))
    stage1_tile = lambda width: pl.BlockSpec(
        (ROW_TILE, width), lambda i: (jnp.maximum(i - 1, 0), 0))
    x1, qkva, qkvb, gates = pl.pallas_call(
        _ffn_in_kernel,
        grid=(n_row_tiles + 1,),
        in_specs=[stage0_tile(D_MODEL), _resident((1, D_MODEL)),
                  _resident((D_MODEL, D_FF)), _resident((D_MODEL, D_FF)),
                  _resident((D_FF, D_MODEL)), _resident((1, D_MODEL)),
                  _resident(w_in.shape)],
        out_specs=[stage0_tile(D_MODEL), stage1_tile(QKV_A_WIDTH), stage1_tile(QKV_B_WIDTH),
                   stage1_tile(GATES_WIDTH)],
        out_shape=[jax.ShapeDtypeStruct((rows, D_MODEL), F32),
                   jax.ShapeDtypeStruct((rows, QKV_A_WIDTH), BF16),
                   jax.ShapeDtypeStruct((rows, QKV_B_WIDTH), BF16),
                   jax.ShapeDtypeStruct((rows, GATES_WIDTH), BF16)],
        scratch_shapes=[pltpu.VMEM((ROW_TILE, D_FF), BF16), pltpu.VMEM((ROW_TILE, D_MODEL), BF16)],
        compiler_params=dense_params,
        name="ffn_in",
    )(x.reshape(rows, D_MODEL), vec(ffn1_norm), ffn1_w_gate.astype(BF16),
      ffn1_w_up.astype(BF16), ffn1_w_down.astype(BF16), vec(mix_norm), w_in.astype(BF16))

    tab_a = (rel_bias.astype(F32) * LOG2_E)[:, _bias_a_table_index()][:, :, None, :]
    n_pairs = A_WIDTH // LANES
    batch_block = lambda width: pl.BlockSpec((None, seq, width), lambda b: (b, 0, 0))
    ffn2_weights = (ffn2_w_gate, ffn2_w_up, ffn2_w_down)
    assert all(w.shape[0] % (batch * BF16_SUBLANES) == 0 for w in ffn2_weights)
    cast_block = lambda w: pl.BlockSpec((w.shape[0] // batch, w.shape[1]), lambda b: (b, 0))
    ya, yb, wg2, wu2, wd2 = pl.pallas_call(
        _attn_kernel,
        grid=(batch,),
        in_specs=[pl.BlockSpec(memory_space=pltpu.SMEM),
                  batch_block(QKV_A_WIDTH), batch_block(QKV_B_WIDTH),
                  _resident((A_HEADS, A_KEY_TILES, 1, 2 * A_QBLOCK)),
                  _resident((B_KV_HEADS, B_GROUP * B_QBLOCK, B_WINDOW))]
                 + [cast_block(w) for w in ffn2_weights],
        out_specs=[batch_block(A_WIDTH), batch_block(B_Q_WIDTH)]
                  + [cast_block(w) for w in ffn2_weights],
        out_shape=[jax.ShapeDtypeStruct((batch, seq, A_WIDTH), BF16),
                   jax.ShapeDtypeStruct((batch, seq, B_Q_WIDTH), BF16)]
                  + [jax.ShapeDtypeStruct(w.shape, BF16) for w in ffn2_weights],
        scratch_shapes=[pltpu.VMEM((n_pairs, A_KEY_TILES, 2 * A_QBLOCK, A_QBLOCK), F32),
                        pltpu.VMEM((B_KV_HEADS, seq, 2 * LANES), BF16),
                        pltpu.VMEM((B_KV_HEADS, seq, LANES), BF16),
                        pltpu.VMEM((2, A_KEY_TILES, 2 * A_QBLOCK, A_QBLOCK), F32)],
        compiler_params=pltpu.CompilerParams(dimension_semantics=("arbitrary",),
                                             vmem_limit_bytes=VMEM_LIMIT_BYTES),
        name="attn",
    )(sinks.astype(F32) * LOG2_E, qkva.reshape(batch, seq, QKV_A_WIDTH),
      qkvb.reshape(batch, seq, QKV_B_WIDTH), tab_a, jnp.asarray(_bias_b_table()), *ffn2_weights)

    out = pl.pallas_call(
        _ffn_out_kernel,
        grid=(n_row_tiles + 1,),
        in_specs=[stage0_tile(D_MODEL), stage0_tile(A_WIDTH), stage0_tile(B_Q_WIDTH),
                  stage0_tile(D_MODEL, col=0), stage0_tile(D_MODEL, col=1),
                  _resident((A_WIDTH, D_MODEL)), _resident((B_Q_WIDTH, D_MODEL)),
                  _resident((D_MODEL, D_MODEL)), _resident((1, D_MODEL)),
                  _resident((D_MODEL, D_FF)), _resident((D_MODEL, D_FF)),
                  _resident((D_FF, D_MODEL)), _resident((1, D_MODEL))],
        out_specs=stage1_tile(D_MODEL),
        out_shape=jax.ShapeDtypeStruct((rows, D_MODEL), F32),
        scratch_shapes=[pltpu.VMEM((ROW_TILE, D_FF), BF16), pltpu.VMEM((ROW_TILE, D_MODEL), F32),
                        pltpu.VMEM((ROW_TILE, D_MODEL), BF16), pltpu.VMEM((ROW_TILE, D_MODEL), BF16)],
        compiler_params=dense_params,
        name="ffn_out",
    )(x1, ya.reshape(rows, A_WIDTH), yb.reshape(rows, B_Q_WIDTH), gates, gates,
      w_proj_a.astype(BF16), w_proj_b.astype(BF16), w_out.astype(BF16), vec(ffn2_norm),
      wg2, wu2, wd2, vec(final_norm))
    return out.reshape(batch, seq, D_MODEL)
```

```python
import functools

import numpy as np
import jax
import jax.numpy as jnp
from jax import lax
from jax.experimental import pallas as pl
from jax.experimental.pallas import tpu as pltpu

F32 = jnp.float32
BF16 = jnp.bfloat16

D_MODEL = 1024
D_FF = 2816
CHUNK = 64
D_HEAD = 64
A_HEADS = 8
A_PREV_CHUNKS = 8
MAX_REL = 128
B_Q_HEADS = 8
B_KV_HEADS = 2
B_GROUP = B_Q_HEADS // B_KV_HEADS
B_PREV_CHUNKS = 2
A_WIDTH = A_HEADS * D_HEAD
B_Q_WIDTH = B_Q_HEADS * D_HEAD
B_KV_WIDTH = B_KV_HEADS * D_HEAD
QKV_A_WIDTH = 3 * A_WIDTH
QKV_B_WIDTH = B_Q_WIDTH + 2 * B_KV_WIDTH
GATES_WIDTH = 2 * D_MODEL
EPS = 1e-6
NEG_INF = -1e30
LOG2_E = float(np.log2(np.e))
QK_SCALE = LOG2_E / float(np.sqrt(D_HEAD))

LANES = 128
BF16_SUBLANES = 16
MXU_DIM = 256
VMEM_LIMIT_BYTES = 56 * 1024 * 1024

ROW_TILE = 512
FF_CHUNK = MXU_DIM
A_QBLOCK = 4 * CHUNK
A_KEY_TILES = A_PREV_CHUNKS * CHUNK // A_QBLOCK + 1
B_QBLOCK = 2 * CHUNK
B_WINDOW = 2 * B_QBLOCK
assert B_GROUP * B_QBLOCK == 2 * A_QBLOCK and B_WINDOW == A_QBLOCK and B_KV_HEADS <= A_KEY_TILES


def _rms_norm(x, gain):
    return x * lax.rsqrt(jnp.mean(x * x, axis=-1, keepdims=True) + EPS) * gain


def _swiglu(h, wg_ref, wu_ref, wd_ref, hid_ref, between_chunks=None):
    between_chunks = between_chunks or {}
    for c in range(D_FF // FF_CHUNK):
        cols = slice(c * FF_CHUNK, (c + 1) * FF_CHUNK)
        lhs = h[...]
        g = jnp.dot(lhs, wg_ref[:, cols], preferred_element_type=F32)
        u = jnp.dot(lhs, wu_ref[:, cols], preferred_element_type=F32)
        hid_ref[:, cols] = (g * jax.nn.sigmoid(g) * u).astype(BF16)
        if c in between_chunks:
            between_chunks[c]()
    return jnp.dot(hid_ref[...], wd_ref[...], preferred_element_type=F32)


def _ffn_in_kernel(x_ref, g1_ref, wg_ref, wu_ref, wd_ref, gm_ref, win_ref,
                   x1_ref, qkva_ref, qkvb_ref, gates_ref, hid_ref, h2_ref):
    i = pl.program_id(0)
    last = pl.num_programs(0) - 1

    def ffn_stage():
        h = _rms_norm(x_ref[...], g1_ref[...]).astype(BF16)
        y = _swiglu(h, wg_ref, wu_ref, wd_ref, hid_ref)
        x1 = x_ref[...] + 0.5 * y
        x1_ref[...] = x1
        h2_ref[...] = _rms_norm(x1, gm_ref[...]).astype(BF16)

    def projection_stage():
        h2 = h2_ref[...]
        col = 0
        for o_ref, q_width in ((qkva_ref, A_WIDTH), (qkvb_ref, B_Q_WIDTH), (gates_ref, 0)):
            for lo, hi, factor in ((0, q_width, QK_SCALE), (q_width, o_ref.shape[1], None)):
                if hi > lo:
                    y = jnp.dot(h2, win_ref[:, col + lo:col + hi], preferred_element_type=F32)
                    o_ref[:, lo:hi] = (y if factor is None else y * factor).astype(BF16)
            col += o_ref.shape[1]

    @pl.when(i == 0)
    def _first():
        ffn_stage()

    @pl.when((i > 0) & (i < last))
    def _steady():
        projection_stage()
        ffn_stage()

    @pl.when(i == last)
    def _last():
        projection_stage()


def _ffn_out_kernel(x1_ref, ya_ref, yb_ref, ga_ref, gb_ref, wpa_ref, wpb_ref, wo_ref,
                    g2_ref, wg_ref, wu_ref, wd_ref, gf_ref, out_ref, hid_ref, x2_ref, h_ref,
                    mg_ref):
    i = pl.program_id(0)
    last = pl.num_programs(0) - 1
    merged_tile = {}

    def gate_columns(c):
        cols = slice(c * MXU_DIM, (c + 1) * MXU_DIM)
        pa = jnp.dot(ya_ref[...], wpa_ref[:, cols], preferred_element_type=F32)
        pb = jnp.dot(yb_ref[...], wpb_ref[:, cols], preferred_element_type=F32)
        mg_ref[:, cols] = (jax.nn.sigmoid(ga_ref[:, cols].astype(F32)) * pa
                           + jax.nn.sigmoid(gb_ref[:, cols].astype(F32)) * pb).astype(BF16)

    def mix_in():
        mixed = jnp.dot(mg_ref[...], wo_ref[...], preferred_element_type=F32)
        x2 = x1_ref[...] + mixed
        merged_tile["x2"] = x2
        merged_tile["h"] = _rms_norm(x2, g2_ref[...]).astype(BF16)

    def store_merged_tile():
        x2_ref[...] = merged_tile["x2"]
        h_ref[...] = merged_tile["h"]

    merge_pieces = [functools.partial(gate_columns, c) for c in range(D_MODEL // MXU_DIM)]
    merge_pieces.append(mix_in)

    def ffn_stage(between_chunks=None):
        y = _swiglu(h_ref, wg_ref, wu_ref, wd_ref, hid_ref, between_chunks)
        x3 = x2_ref[...] + 0.5 * y
        out_ref[...] = _rms_norm(x3, gf_ref[...])

    @pl.when(i == 0)
    def _first():
        for piece in merge_pieces:
            piece()
        store_merged_tile()

    @pl.when((i > 0) & (i < last))
    def _steady():
        ffn_stage(dict(zip((0, 1, 2, 3, 5), merge_pieces)))
        store_merged_tile()

    @pl.when(i == last)
    def _last():
        ffn_stage()


def _dot_nt(a, b):
    return lax.dot_general(a, b, (((1,), (1,)), ((), ())), preferred_element_type=F32)


def _attn_kernel(sinks_ref, qkva_ref, qkvb_ref, tab_ref, bias_b_ref, wg2_ref, wu2_ref, wd2_ref,
                 ya_ref, yb_ref, wg2_bf16_ref, wu2_bf16_ref, wd2_bf16_ref,
                 bias_a_ref, vext_b_ref, kdup_ref, s_ref):
    seq = qkva_ref.shape[0]
    n_pairs = A_WIDTH // LANES
    low = lax.broadcasted_iota(jnp.int32, (1, LANES), 1) < D_HEAD
    ones = jnp.ones((seq, LANES), BF16)
    for src, dst in ((wg2_ref, wg2_bf16_ref), (wu2_ref, wu2_bf16_ref), (wd2_ref, wd2_bf16_ref)):
        dst[...] = src[...].astype(BF16)

    @pl.when(pl.program_id(0) == 0)
    def _build_bias_a():
        qc = lax.broadcasted_iota(jnp.int32, (A_QBLOCK, A_QBLOCK), 0) // CHUNK
        kc = lax.broadcasted_iota(jnp.int32, (A_QBLOCK, A_QBLOCK), 1) // CHUNK
        for h in range(A_HEADS):
            for d in range(A_KEY_TILES):
                row = jnp.broadcast_to(tab_ref[h, d], (A_QBLOCK, 2 * A_QBLOCK))
                toeplitz = pltpu.roll(row, 0, 1, stride=1, stride_axis=0)[:, :A_QBLOCK]
                chunk_diff = d * (A_QBLOCK // CHUNK) + qc - kc
                visible = (chunk_diff >= 0) & (chunk_diff <= A_PREV_CHUNKS)
                bias_a_ref[h // 2, d, (h % 2) * A_QBLOCK:(h % 2 + 1) * A_QBLOCK, :] = jnp.where(
                    visible, toeplitz, NEG_INF)

    def mixer_a_scores(slot, j, m):
        q = qkva_ref[m * A_QBLOCK:(m + 1) * A_QBLOCK, j * LANES:(j + 1) * LANES]
        zero = jnp.zeros_like(q)
        qs = jnp.concatenate([jnp.where(low, q, zero), jnp.where(low, zero, q)], axis=0)
        kcols = slice(A_WIDTH + j * LANES, A_WIDTH + (j + 1) * LANES)
        dists = [d for d in range(A_KEY_TILES - 1, -1, -1) if m - d >= 0]
        key_rows = [slice((m - d) * A_QBLOCK, (m - d + 1) * A_QBLOCK) for d in dists]
        tile_max = []
        for t, (d, rows) in enumerate(zip(dists, key_rows)):
            logits = _dot_nt(qs, qkva_ref[rows, kcols]) + bias_a_ref[j, d]
            s_ref[slot, t] = logits
            tile_max.append(jnp.max(logits, axis=-1, keepdims=True))
        return functools.reduce(jnp.maximum, tile_max), key_rows

    def mixer_a_finish(slot, j, m, staged):
        mx, key_rows = staged
        acc = None
        vcols = slice(2 * A_WIDTH + j * LANES, 2 * A_WIDTH + (j + 1) * LANES)
        for t, rows in enumerate(key_rows):
            v_and_ones = jnp.concatenate([qkva_ref[rows, vcols], ones[:A_QBLOCK]], axis=1)
            pv = jnp.dot(jnp.exp2(s_ref[slot, t] - mx).astype(BF16), v_and_ones,
                         preferred_element_type=F32)
            acc = pv if acc is None else acc + pv
        o = acc[:, :LANES] / acc[:, LANES:]
        ya_ref[m * A_QBLOCK:(m + 1) * A_QBLOCK, j * LANES:(j + 1) * LANES] = jnp.where(
            low, o[:A_QBLOCK], o[A_QBLOCK:]).astype(BF16)

    def mixer_b_scores(slot, m):
        krows = slice(max(m - 1, 0) * B_QBLOCK, (m + 1) * B_QBLOCK)
        width = krows.stop - krows.start
        maxima = []
        for g in range(B_KV_HEADS):
            tiles = [qkvb_ref[m * B_QBLOCK:(m + 1) * B_QBLOCK, c * LANES:(c + 1) * LANES]
                     for c in range(g * B_GROUP // 2, (g + 1) * B_GROUP // 2)]
            zero = jnp.zeros_like(tiles[0])
            qs = jnp.concatenate([piece for t in tiles
                                  for piece in (jnp.where(low, t, zero), jnp.where(low, zero, t))],
                                 axis=0)
            logits = _dot_nt(qs, kdup_ref[g, krows, :]) + bias_b_ref[g, :, B_WINDOW - width:]
            s_ref[slot, g, :, :width] = logits
            maxima.append(jnp.maximum(jnp.max(logits, axis=-1, keepdims=True), sink[g]))
        return maxima

    def mixer_b_finish(slot, m, staged):
        qrows = slice(m * B_QBLOCK, (m + 1) * B_QBLOCK)
        krows = slice(max(m - 1, 0) * B_QBLOCK, (m + 1) * B_QBLOCK)
        width = krows.stop - krows.start
        for g, mx in enumerate(staged):
            p = jnp.exp2(s_ref[slot, g, :, :width] - jnp.concatenate([mx] * (width // LANES), axis=1))
            pv = jnp.dot(p.astype(BF16), vext_b_ref[g, krows, :], preferred_element_type=F32)
            o = pv[:, :LANES] / (pv[:, LANES:] + jnp.exp2(sink[g] - mx))
            for t in range(B_GROUP // 2):
                c = g * B_GROUP // 2 + t
                even, odd = (o[(2 * t + h) * B_QBLOCK:(2 * t + h + 1) * B_QBLOCK] for h in (0, 1))
                yb_ref[qrows, c * LANES:(c + 1) * LANES] = jnp.where(low, even, odd).astype(BF16)

    for g in range(B_KV_HEADS):
        for src, dst in ((qkvb_ref[:, B_Q_WIDTH:B_Q_WIDTH + LANES], kdup_ref.at[g]),
                         (qkvb_ref[:, B_Q_WIDTH + LANES:], vext_b_ref.at[g])):
            half = src[:, g * D_HEAD:(g + 1) * D_HEAD]
            dst[:, :LANES] = jnp.concatenate([half, half], axis=1)
        vext_b_ref[g, :, LANES:] = ones
    sink = [jnp.concatenate([jnp.full((B_QBLOCK, LANES), sinks_ref[g * B_GROUP + c], F32)
                             for c in range(B_GROUP)], axis=0) for g in range(B_KV_HEADS)]

    b_blocks_per_pair = seq // B_QBLOCK // n_pairs
    blocks = []
    for j in range(n_pairs):
        blocks += [(functools.partial(mixer_a_scores, j=j, m=m),
                    functools.partial(mixer_a_finish, j=j, m=m)) for m in range(seq // A_QBLOCK)]
        blocks += [(functools.partial(mixer_b_scores, m=m), functools.partial(mixer_b_finish, m=m))
                   for m in range(j * b_blocks_per_pair, (j + 1) * b_blocks_per_pair)]
    staged = blocks[0][0](slot=0)
    for n, (_, finish) in enumerate(blocks):
        next_staged = blocks[n + 1][0](slot=(n + 1) % 2) if n + 1 < len(blocks) else None
        finish(slot=n % 2, staged=staged)
        staged = next_staged


def _resident(shape):
    return pl.BlockSpec(shape, lambda *_: (0,) * len(shape), pipeline_mode=pl.Buffered(1))


def _bias_a_table_index():
    u = np.arange(2 * A_QBLOCK)
    key_minus_query = np.where(u <= A_QBLOCK, u, u - 2 * A_QBLOCK)
    rel = np.arange(A_KEY_TILES)[:, None] * A_QBLOCK - key_minus_query[None, :]
    return np.clip(rel, -(CHUNK - 1), MAX_REL) + (CHUNK - 1)


def _bias_b_table():
    slopes = np.array([2.0 ** (-8.0 * (h + 1) / B_Q_HEADS) for h in range(B_Q_HEADS)], np.float32)
    i = np.arange(B_QBLOCK)[:, None]
    j = np.arange(B_WINDOW)[None, :]
    rel = i - j + B_QBLOCK
    chunk_diff = i // CHUNK - j // CHUNK + B_QBLOCK // CHUNK
    vis = (chunk_diff >= 0) & (chunk_diff <= B_PREV_CHUNKS)
    dist = np.abs(rel).astype(np.float32)
    out = np.empty((B_KV_HEADS, B_GROUP * B_QBLOCK, B_WINDOW), np.float32)
    for g in range(B_KV_HEADS):
        for c in range(B_GROUP):
            out[g, c * B_QBLOCK:(c + 1) * B_QBLOCK] = np.where(
                vis, -slopes[g * B_GROUP + c] * dist * LOG2_E, NEG_INF)
    return out


def kernel(x, ffn1_norm, ffn1_w_gate, ffn1_w_up, ffn1_w_down, mix_norm, w_in, rel_bias, sinks,
           w_proj_a, w_proj_b, w_out, ffn2_norm, ffn2_w_gate, ffn2_w_up, ffn2_w_down, final_norm):
    batch, seq, d = x.shape
    rows = batch * seq
    assert d == D_MODEL and rows % ROW_TILE == 0 and seq % A_QBLOCK == 0
    n_row_tiles = rows // ROW_TILE
    vec = lambda g: g.reshape(1, D_MODEL).astype(F32)

    dense_params = pltpu.CompilerParams(dimension_semantics=("arbitrary",),
                                        vmem_limit_bytes=VMEM_LIMIT_BYTES)

    stage0_tile = lambda width, col=0: pl.BlockSpec(
        (ROW_TILE, width), lambda i: (jnp.minimum(i, n_row_tiles - 1), col))
    stage1_tile = lambda width: pl.BlockSpec(
        (ROW_TILE, width), lambda i: (jnp.maximum(i - 1, 0), 0))
    x1, qkva, qkvb, gates = pl.pallas_call(
        _ffn_in_kernel,
        grid=(n_row_tiles + 1,),
        in_specs=[stage0_tile(D_MODEL), _resident((1, D_MODEL)),
                  _resident((D_MODEL, D_FF)), _resident((D_MODEL, D_FF)),
                  _resident((D_FF, D_MODEL)), _resident((1, D_MODEL)),
                  _resident(w_in.shape)],
        out_specs=[stage0_tile(D_MODEL), stage1_tile(QKV_A_WIDTH), stage1_tile(QKV_B_WIDTH),
                   stage1_tile(GATES_WIDTH)],
        out_shape=[jax.ShapeDtypeStruct((rows, D_MODEL), F32),
                   jax.ShapeDtypeStruct((rows, QKV_A_WIDTH), BF16),
                   jax.ShapeDtypeStruct((rows, QKV_B_WIDTH), BF16),
                   jax.ShapeDtypeStruct((rows, GATES_WIDTH), BF16)],
        scratch_shapes=[pltpu.VMEM((ROW_TILE, D_FF), BF16), pltpu.VMEM((ROW_TILE, D_MODEL), BF16)],
        compiler_params=dense_params,
        name="ffn_in",
    )(x.reshape(rows, D_MODEL), vec(ffn1_norm), ffn1_w_gate.astype(BF16),
      ffn1_w_up.astype(BF16), ffn1_w_down.astype(BF16), vec(mix_norm), w_in.astype(BF16))

    tab_a = (rel_bias.astype(F32) * LOG2_E)[:, _bias_a_table_index()][:, :, None, :]
    n_pairs = A_WIDTH // LANES
    batch_block = lambda width: pl.BlockSpec((None, seq, width), lambda b: (b, 0, 0))
    ffn2_weights = (ffn2_w_gate, ffn2_w_up, ffn2_w_down)
    assert all(w.shape[0] % (batch * BF16_SUBLANES) == 0 for w in ffn2_weights)
    cast_block = lambda w: pl.BlockSpec((w.shape[0] // batch, w.shape[1]), lambda b: (b, 0))
    ya, yb, wg2, wu2, wd2 = pl.pallas_call(
        _attn_kernel,
        grid=(batch,),
        in_specs=[pl.BlockSpec(memory_space=pltpu.SMEM),
                  batch_block(QKV_A_WIDTH), batch_block(QKV_B_WIDTH),
                  _resident((A_HEADS, A_KEY_TILES, 1, 2 * A_QBLOCK)),
                  _resident((B_KV_HEADS, B_GROUP * B_QBLOCK, B_WINDOW))]
                 + [cast_block(w) for w in ffn2_weights],
        out_specs=[batch_block(A_WIDTH), batch_block(B_Q_WIDTH)]
                  + [cast_block(w) for w in ffn2_weights],
        out_shape=[jax.ShapeDtypeStruct((batch, seq, A_WIDTH), BF16),
                   jax.ShapeDtypeStruct((batch, seq, B_Q_WIDTH), BF16)]
                  + [jax.ShapeDtypeStruct(w.shape, BF16) for w in ffn2_weights],
        scratch_shapes=[pltpu.VMEM((n_pairs, A_KEY_TILES, 2 * A_QBLOCK, A_QBLOCK), F32),
                        pltpu.VMEM((B_KV_HEADS, seq, 2 * LANES), BF16),
                        pltpu.VMEM((B_KV_HEADS, seq, LANES), BF16),
                        pltpu.VMEM((2, A_KEY_TILES, 2 * A_QBLOCK, A_QBLOCK), F32)],
        compiler_params=pltpu.CompilerParams(dimension_semantics=("arbitrary",),
                                             vmem_limit_bytes=VMEM_LIMIT_BYTES),
        name="attn",
    )(sinks.astype(F32) * LOG2_E, qkva.reshape(batch, seq, QKV_A_WIDTH),
      qkvb.reshape(batch, seq, QKV_B_WIDTH), tab_a, jnp.asarray(_bias_b_table()), *ffn2_weights)

    out = pl.pallas_call(
        _ffn_out_kernel,
        grid=(n_row_tiles + 1,),
        in_specs=[stage0_tile(D_MODEL), stage0_tile(A_WIDTH), stage0_tile(B_Q_WIDTH),
                  stage0_tile(D_MODEL, col=0), stage0_tile(D_MODEL, col=1),
                  _resident((A_WIDTH, D_MODEL)), _resident((B_Q_WIDTH, D_MODEL)),
                  _resident((D_MODEL, D_MODEL)), _resident((1, D_MODEL)),
                  _resident((D_MODEL, D_FF)), _resident((D_MODEL, D_FF)),
                  _resident((D_FF, D_MODEL)), _resident((1, D_MODEL))],
        out_specs=stage1_tile(D_MODEL),
        out_shape=jax.ShapeDtypeStruct((rows, D_MODEL), F32),
        scratch_shapes=[pltpu.VMEM((ROW_TILE, D_FF), BF16), pltpu.VMEM((ROW_TILE, D_MODEL), F32),
                        pltpu.VMEM((ROW_TILE, D_MODEL), BF16), pltpu.VMEM((ROW_TILE, D_MODEL), BF16)],
        compiler_params=dense_params,
        name="ffn_out",
    )(x1, ya.reshape(rows, A_WIDTH), yb.reshape(rows, B_Q_WIDTH), gates, gates,
      w_proj_a.astype(BF16), w_proj_b.astype(BF16), w_out.astype(BF16), vec(ffn2_norm),
      wg2, wu2, wd2, vec(final_norm))
    return out.reshape(batch, seq, D_MODEL)
```
